```python
import math
import jax, jax.numpy as jnp
from jax import lax
import numpy as np

D_MODEL = 2048
BATCH = 2
SEQ = 16384
DEPTH = 4

N_MIXERS = 4
GRID_W = 64
Q_BLOCK = 128
ROWS_PER_BLOCK = Q_BLOCK // GRID_W
ROPE_THETA = 10000.0
NEG_INF = -1e30
MLA_HEADS = D_MODEL // 128
MLA_NOPE = 128
MLA_ROPE = 64
MLA_V = 128
MLA_Q_RANK = D_MODEL // 4
MLA_KV_RANK = D_MODEL // 4
GQA_HEADS = D_MODEL // 128
GQA_KV_HEADS = 4
GQA_HD = 128
NA_HEADS = D_MODEL // 128
NA_HD = 128
NA_KH_MAX = 8
NA_KW = 16
DIFF_HD = 128
DIFF_HEADS = D_MODEL // (2 * DIFF_HD)
FFN_HIDDEN = -(-8 * D_MODEL // (3 * 256)) * 256
DEEPNORM_ALPHA = (2.0 * DEPTH) ** 0.25
DEEPNORM_BETA = (8.0 * DEPTH) ** -0.25

kernel_name = 'hybrid_mla_gqa_na_diff_encoder'


def layer_norm(x, g, b, eps=1e-5):
    xf = x.astype(jnp.float32)
    mu = jnp.mean(xf, axis=-1, keepdims=True)
    var = jnp.mean(jnp.square(xf - mu), axis=-1, keepdims=True)
    y = (xf - mu) * lax.rsqrt(var + eps) * g.astype(jnp.float32) + b.astype(jnp.float32)
    return y.astype(x.dtype)


def rms_norm(x, g, eps=1e-6):
    xf = x.astype(jnp.float32)
    y = xf * lax.rsqrt(jnp.mean(jnp.square(xf), axis=-1, keepdims=True) + eps) * g.astype(jnp.float32)
    return y.astype(x.dtype)


def rope_cos_sin(pos, dim):
    inv = 1.0 / (ROPE_THETA ** (jnp.arange(0, dim, 2, dtype=jnp.float32) / dim))
    ang = pos.astype(jnp.float32)[:, None] * inv[None, :]
    return jnp.cos(ang), jnp.sin(ang)


def apply_rope(x, cos, sin):
    half = x.shape[-1] // 2
    x1, x2 = x[..., :half], x[..., half:]
    c, s = cos.astype(x.dtype), sin.astype(x.dtype)
    return jnp.concatenate([x1 * c - x2 * s, x1 * s + x2 * c], axis=-1)


def alibi_slopes(n):
    return jnp.exp2(-8.0 * jnp.arange(1, n + 1, dtype=jnp.float32) / n)


def blocked_attention(q, k, v, scale):
    B, S, Hk, G, Dq = q.shape
    nblk = S // Q_BLOCK
    qb = jnp.moveaxis(q.reshape(B, nblk, Q_BLOCK, Hk, G, Dq), 1, 0)

    def one(qblk):
        s = jnp.einsum('bqhgd,bkhd->bhgqk', qblk, k, preferred_element_type=jnp.float32) * scale
        p = jax.nn.softmax(s, axis=-1).astype(v.dtype)
        return jnp.einsum('bhgqk,bkhd->bqhgd', p, v)

    o = lax.map(one, qb)
    return jnp.moveaxis(o, 0, 1).reshape(B, S, Hk, G, v.shape[-1])


def mla_mixer(x, w_in, q_norm, w_q_up, kv_norm, w_kv_up, w_o):
    B, S, _ = x.shape
    c = x @ w_in
    cq, ckv, k_rope = jnp.split(c, [MLA_Q_RANK, MLA_Q_RANK + MLA_KV_RANK], axis=-1)
    q = (rms_norm(cq, q_norm) @ w_q_up).reshape(B, S, MLA_HEADS, MLA_NOPE + MLA_ROPE)
    kv = (rms_norm(ckv, kv_norm) @ w_kv_up).reshape(B, S, MLA_HEADS, MLA_NOPE + MLA_V)
    cos, sin = rope_cos_sin(jnp.arange(S), MLA_ROPE)
    q_rope = apply_rope(q[..., MLA_NOPE:], cos[:, None], sin[:, None])
    k_rope = apply_rope(k_rope, cos, sin)
    q = jnp.concatenate([q[..., :MLA_NOPE], q_rope], axis=-1)
    k = jnp.concatenate([kv[..., :MLA_NOPE],
                         jnp.broadcast_to(k_rope[:, :, None, :], (B, S, MLA_HEADS, MLA_ROPE))], axis=-1)
    v = kv[..., MLA_NOPE:]
    o = blocked_attention(q[:, :, :, None, :], k, v, (MLA_NOPE + MLA_ROPE) ** -0.5)
    return o.reshape(B, S, MLA_HEADS * MLA_V) @ w_o


def gqa_axial_mixer(x, w_qkv, q_norm, k_norm, w_o):
    B, S, _ = x.shape
    G = GQA_HEADS // GQA_KV_HEADS
    q, k, v = jnp.split(x @ w_qkv, [GQA_HEADS * GQA_HD, (GQA_HEADS + GQA_KV_HEADS) * GQA_HD], axis=-1)
    q = rms_norm(q.reshape(B, S, GQA_HEADS, GQA_HD), q_norm)
    k = rms_norm(k.reshape(B, S, GQA_KV_HEADS, GQA_HD), k_norm)
    v = v.reshape(B, S, GQA_KV_HEADS, GQA_HD)
    pos = jnp.arange(S)
    half = GQA_HD // 2
    cr, sr = rope_cos_sin(pos // GRID_W, half)
    cc, sc = rope_cos_sin(pos % GRID_W, half)

    def axial(t):
        return jnp.concatenate([apply_rope(t[..., :half], cr[:, None], sr[:, None]),
                                apply_rope(t[..., half:], cc[:, None], sc[:, None])], axis=-1)

    q = axial(q).reshape(B, S, GQA_KV_HEADS, G, GQA_HD)
    k = axial(k)
    o = blocked_attention(q, k, v, GQA_HD ** -0.5)
    return o.reshape(B, S, GQA_HEADS * GQA_HD) @ w_o


def na_mixer(x, w_qkv, rpb, w_o):
    B, S, _ = x.shape
    R = S // GRID_W
    kh = min(NA_KH_MAX, R)
    nb_rows = min(kh + ROWS_PER_BLOCK - 1, R)
    nk = nb_rows * GRID_W
    q, k, v = jnp.split(x @ w_qkv, 3, axis=-1)
    q = q.reshape(B, S, NA_HEADS, NA_HD)
    k5 = k.reshape(B, R, GRID_W, NA_HEADS, NA_HD)
    v5 = v.reshape(B, R, GRID_W, NA_HEADS, NA_HD)
    nblk = S // Q_BLOCK
    qb = jnp.moveaxis(q.reshape(B, nblk, Q_BLOCK, NA_HEADS, NA_HD), 1, 0)
    lq = jnp.arange(Q_BLOCK)
    lk = jnp.arange(nk)
    qcol = lq % GRID_W
    kcol = lk % GRID_W
    cs = jnp.clip(qcol - NA_KW // 2, 0, GRID_W - NA_KW)
    col_ok = (kcol[None, :] >= cs[:, None]) & (kcol[None, :] < cs[:, None] + NA_KW)
    dcol = jnp.clip(kcol[None, :] - qcol[:, None] + NA_KW - 1, 0, 2 * NA_KW - 2)
    scale = NA_HD ** -0.5

    def one(args):
        qblk, j = args
        r0 = j * ROWS_PER_BLOCK
        band = jnp.minimum(jnp.clip(r0 - kh // 2, 0, R - kh), R - nb_rows)
        kb = lax.dynamic_slice_in_dim(k5, band, nb_rows, axis=1).reshape(B, nk, NA_HEADS, NA_HD)
        vb = lax.dynamic_slice_in_dim(v5, band, nb_rows, axis=1).reshape(B, nk, NA_HEADS, NA_HD)
        qrow = r0 + lq // GRID_W
        krow = band + lk // GRID_W
        rs = jnp.clip(qrow - kh // 2, 0, R - kh)
        ok = col_ok & (krow[None, :] >= rs[:, None]) & (krow[None, :] < rs[:, None] + kh)
        drow = jnp.clip(krow[None, :] - qrow[:, None] + NA_KH_MAX - 1, 0, 2 * NA_KH_MAX - 2)
        bias = rpb[:, drow, dcol].astype(jnp.float32)
        s = jnp.einsum('bqhd,bkhd->bhqk', qblk, kb, preferred_element_type=jnp.float32) * scale + bias[None]
        s = jnp.where(ok[None, None], s, NEG_INF)
        p = jax.nn.softmax(s, axis=-1).astype(vb.dtype)
        return jnp.einsum('bhqk,bkhd->bqhd', p, vb)

    o = lax.map(one, (qb, jnp.arange(nblk)))
    return jnp.moveaxis(o, 0, 1).reshape(B, S, NA_HEADS * NA_HD) @ w_o


def diff_mixer(x, w_qkv, lq1, lk1, lq2, lk2, subln, w_o, lam_init):
    B, S, _ = x.shape
    f32 = jnp.float32
    q, k, v = jnp.split(x @ w_qkv, 3, axis=-1)
    q = q.reshape(B, S, DIFF_HEADS, 2, DIFF_HD)
    k = k.reshape(B, S, DIFF_HEADS, 2, DIFF_HD)
    v = v.reshape(B, S, DIFF_HEADS, 2 * DIFF_HD)
    lam = (jnp.exp(jnp.sum(lq1.astype(f32) * lk1.astype(f32)))
           - jnp.exp(jnp.sum(lq2.astype(f32) * lk2.astype(f32))) + lam_init)
    slopes = alibi_slopes(DIFF_HEADS)
    kpos = jnp.arange(S)
    nblk = S // Q_BLOCK
    qb = jnp.moveaxis(q.reshape(B, nblk, Q_BLOCK, DIFF_HEADS, 2, DIFF_HD), 1, 0)
    scale = DIFF_HD ** -0.5

    def one(args):
        qblk, j = args
        qpos = j * Q_BLOCK + jnp.arange(Q_BLOCK)
        dist = jnp.abs(qpos[:, None] - kpos[None, :]).astype(f32)
        s = jnp.einsum('bqhcd,bkhcd->bhcqk', qblk, k, preferred_element_type=f32) * scale
        s = s - slopes[None, :, None, None, None] * dist
        p = jax.nn.softmax(s, axis=-1)
        a = p[:, :, 0] - lam * p[:, :, 1]
        return jnp.einsum('bhqk,bkhd->bqhd', a.astype(v.dtype), v)

    o = lax.map(one, (qb, jnp.arange(nblk)))
    o = jnp.moveaxis(o, 0, 1).reshape(B, S, DIFF_HEADS, 2 * DIFF_HD)
    o = rms_norm(o, subln) * (1.0 - lam_init)
    return o.reshape(B, S, DIFF_HEADS * 2 * DIFF_HD) @ w_o


def swiglu(x, w_gate, w_up, w_down):
    return (jax.nn.silu(x @ w_gate) * (x @ w_up)) @ w_down


def setup_inputs(seed: int = 0) -> dict:
    key = jax.random.key(seed)
    ks = iter(jax.random.split(key, 40))
    f32 = jnp.float32

    def w(shape, fan_in, scale=1.0):
        return jax.random.normal(next(ks), shape, f32) * (scale * fan_in ** -0.5)

    def gain(shape):
        return 1.0 + 0.02 * jax.random.normal(next(ks), shape, f32)

    def small(shape, s):
        return s * jax.random.normal(next(ks), shape, f32)

    D, F, L = D_MODEL, FFN_HIDDEN, DEPTH
    beta = DEEPNORM_BETA
    return {
        'x': jax.random.normal(next(ks), (BATCH, SEQ, D), f32),
        'mla_w_in': w((D, MLA_Q_RANK + MLA_KV_RANK + MLA_ROPE), D),
        'mla_q_norm': gain((MLA_Q_RANK,)),
        'mla_w_q_up': w((MLA_Q_RANK, MLA_HEADS * (MLA_NOPE + MLA_ROPE)), MLA_Q_RANK),
        'mla_kv_norm': gain((MLA_KV_RANK,)),
        'mla_w_kv_up': w((MLA_KV_RANK, MLA_HEADS * (MLA_NOPE + MLA_V)), MLA_KV_RANK),
        'mla_w_o': w((MLA_HEADS * MLA_V, D), MLA_HEADS * MLA_V, beta),
        'gqa_w_qkv': w((D, (GQA_HEADS + 2 * GQA_KV_HEADS) * GQA_HD), D),
        'gqa_q_norm': gain((GQA_HD,)),
        'gqa_k_norm': gain((GQA_HD,)),
        'gqa_w_o': w((GQA_HEADS * GQA_HD, D), GQA_HEADS * GQA_HD, beta),
        'na_w_qkv': w((D, 3 * NA_HEADS * NA_HD), D),
        'na_rpb': small((NA_HEADS, 2 * NA_KH_MAX - 1, 2 * NA_KW - 1), 0.1),
        'na_w_o': w((NA_HEADS * NA_HD, D), NA_HEADS * NA_HD, beta),
        'diff_w_qkv': w((D, 3 * DIFF_HEADS * 2 * DIFF_HD), D),
        'diff_lambda_q1': small((DIFF_HD,), 0.1),
        'diff_lambda_k1': small((DIFF_HD,), 0.1),
        'diff_lambda_q2': small((DIFF_HD,), 0.1),
        'diff_lambda_k2': small((DIFF_HD,), 0.1),
        'diff_subln': gain((2 * DIFF_HD,)),
        'diff_w_o': w((DIFF_HEADS * 2 * DIFF_HD, D), DIFF_HEADS * 2 * DIFF_HD, beta),
        'ffn_w_gate': w((L, D, F), D),
        'ffn_w_up': w((L, D, F), D),
        'ffn_w_down': w((L, F, D), F, beta),
        'ln_mix_g': gain((L, D)),
        'ln_mix_b': small((L, D), 0.02),
        'ln_ffn_g': gain((L, D)),
        'ln_ffn_b': small((L, D), 0.02),
    }


def reference(x, mla_w_in, mla_q_norm, mla_w_q_up, mla_kv_norm, mla_w_kv_up, mla_w_o,
              gqa_w_qkv, gqa_q_norm, gqa_k_norm, gqa_w_o,
              na_w_qkv, na_rpb, na_w_o,
              diff_w_qkv, diff_lambda_q1, diff_lambda_k1, diff_lambda_q2, diff_lambda_k2, diff_subln, diff_w_o,
              ffn_w_gate, ffn_w_up, ffn_w_down, ln_mix_g, ln_mix_b, ln_ffn_g, ln_ffn_b):
    h = x
    for i in range(DEPTH):
        m = i % N_MIXERS
        if m == 0:
            y = mla_mixer(h, mla_w_in, mla_q_norm, mla_w_q_up, mla_kv_norm, mla_w_kv_up, mla_w_o)
        elif m == 1:
            y = gqa_axial_mixer(h, gqa_w_qkv, gqa_q_norm, gqa_k_norm, gqa_w_o)
        elif m == 2:
            y = na_mixer(h, na_w_qkv, na_rpb, na_w_o)
        else:
            lam_init = 0.8 - 0.6 * math.exp(-0.3 * i)
            y = diff_mixer(h, diff_w_qkv, diff_lambda_q1, diff_lambda_k1, diff_lambda_q2, diff_lambda_k2,
                           diff_subln, diff_w_o, lam_init)
        h = layer_norm(DEEPNORM_ALPHA * h + y, ln_mix_g[i], ln_mix_b[i])
        h = layer_norm(DEEPNORM_ALPHA * h + swiglu(h, ffn_w_gate[i], ffn_w_up[i], ffn_w_down[i]),
                       ln_ffn_g[i], ln_ffn_b[i])
    return h
```

```python
import functools
import math

import jax
import jax.numpy as jnp
from jax import lax
from jax.experimental import pallas as pl
from jax.experimental.pallas import tpu as pltpu

F32 = jnp.float32
BF16 = jnp.bfloat16

LANE = 128
VMEM_LIMIT_BYTES = 56 * 1024 * 1024

GRID_W = 64
Q_BLOCK = 128
ROPE_THETA = 10000.0
NEG_INF = -1e30
MLA_NOPE, MLA_ROPE, MLA_V = 128, 64, 128
GQA_KV_HEADS = 4
NA_KH_MAX, NA_KW = 8, 16
NA_WIN_BLOCKS = 5
DIFF_HD = 128
LN_EPS = 1e-5
RMS_EPS = 1e-6


def _cparams(sem):
    return pltpu.CompilerParams(dimension_semantics=sem, vmem_limit_bytes=VMEM_LIMIT_BYTES)


def _proj_kernel(*refs, n_groups, norm, rope, scale):
    it = iter(refs)
    x_ref, w_ref = next(it), next(it)
    g_ref = next(it) if norm is not None else None
    tabs = (next(it), next(it), next(it)) if rope is not None else None
    o_ref = next(it)
    acc = jnp.dot(x_ref[...].astype(BF16), w_ref[...], preferred_element_type=F32)
    if norm == "full":
        ms = jnp.mean(acc * acc, axis=-1, keepdims=True)
        acc = acc * lax.rsqrt(ms + RMS_EPS) * g_ref[...]
    for g in range(n_groups):
        a = acc[:, g * LANE:(g + 1) * LANE]
        if norm == "group":
            ms = jnp.mean(a * a, axis=-1, keepdims=True)
            a = a * lax.rsqrt(ms + RMS_EPS) * g_ref[...]
        if rope == "all" or (rope == "odd" and g % 2 == 1):
            c, s1, s2 = tabs[0][...], tabs[1][...], tabs[2][...]
            a = a * c + pltpu.roll(a, 96, 1) * s1 + pltpu.roll(a, 32, 1) * s2
        if scale != 1.0:
            a = a * scale
        o_ref[:, g * LANE:(g + 1) * LANE] = a.astype(o_ref.dtype)


def _proj(x, w, *, seq, norm=None, gain=None, rope=None, tabs=None, scale=1.0, bm=1024, bn=None):
    T, K = x.shape
    N = w.shape[1]
    if bn is None:
        bn = N if N <= 1024 else 1024
    bm = min(bm, seq)
    assert T % bm == 0 and seq % bm == 0 and N % bn == 0 and bn % LANE == 0
    if norm == "full":
        assert bn == N
    if rope == "odd":
        assert (bn // LANE) % 2 == 0
    in_specs = [pl.BlockSpec((bm, K), lambda i, j: (i, 0)),
                pl.BlockSpec((K, bn), lambda i, j: (0, j))]
    args = [x, w]
    if norm == "full":
        in_specs.append(pl.BlockSpec((1, bn), lambda i, j: (0, 0)))
        args.append(gain.reshape(1, N).astype(F32))
    elif norm == "group":
        in_specs.append(pl.BlockSpec((1, LANE), lambda i, j: (0, 0)))
        args.append(gain.reshape(1, LANE).astype(F32))
    if rope is not None:
        nseq = seq // bm
        for t in tabs:
            in_specs.append(pl.BlockSpec((bm, LANE), lambda i, j: (i % nseq, 0)))
            args.append(t)
    kern = functools.partial(_proj_kernel, n_groups=bn // LANE, norm=norm, rope=rope, scale=scale)
    return pl.pallas_call(
        kern,
        grid=(T // bm, N // bn),
        in_specs=in_specs,
        out_specs=pl.BlockSpec((bm, bn), lambda i, j: (i, j)),
        out_shape=jax.ShapeDtypeStruct((T, N), BF16),
        compiler_params=_cparams(("parallel", "parallel")),
        name="proj",
    )(*args)


def _mm_ln_kernel(a_ref, w_ref, h_ref, g_ref, b_ref, of_ref, ob_ref, acc_ref, *, alpha, nk):
    k = pl.program_id(1)
    part = jnp.dot(a_ref[...], w_ref[...], preferred_element_type=F32)

    @pl.when(k == 0)
    def _():
        acc_ref[...] = part

    @pl.when(k > 0)
    def _():
        acc_ref[...] += part

    @pl.when(k == nk - 1)
    def _():
        y = alpha * h_ref[...] + acc_ref[...]
        mu = jnp.mean(y, axis=-1, keepdims=True)
        d = y - mu
        var = jnp.mean(d * d, axis=-1, keepdims=True)
        out = d * lax.rsqrt(var + LN_EPS) * g_ref[...] + b_ref[...]
        of_ref[...] = out
        ob_ref[...] = out.astype(BF16)


def _pick_bk(K):
    for bk in (1024, 1408, 512, 256, 128):
        if K % bk == 0:
            return bk
    raise ValueError(K)


def _mm_ln(a, w, h, g, b, *, alpha, bm=512):
    T, K = a.shape
    D = w.shape[1]
    bk = _pick_bk(K)
    nk = K // bk
    assert T % bm == 0
    kern = functools.partial(_mm_ln_kernel, alpha=alpha, nk=nk)
    return pl.pallas_call(
        kern,
        grid=(T // bm, nk),
        in_specs=[pl.BlockSpec((bm, bk), lambda i, k: (i, k)),
                  pl.BlockSpec((bk, D), lambda i, k: (k, 0)),
                  pl.BlockSpec((bm, D), lambda i, k: (i, 0)),
                  pl.BlockSpec((1, D), lambda i, k: (0, 0)),
                  pl.BlockSpec((1, D), lambda i, k: (0, 0))],
        out_specs=[pl.BlockSpec((bm, D), lambda i, k: (i, 0)),
                   pl.BlockSpec((bm, D), lambda i, k: (i, 0))],
        out_shape=[jax.ShapeDtypeStruct((T, D), F32), jax.ShapeDtypeStruct((T, D), BF16)],
        scratch_shapes=[pltpu.VMEM((bm, D), F32)],
        compiler_params=_cparams(("parallel", "arbitrary")),
        name="mm_ln",
    )(a, w, h, g.reshape(1, D).astype(F32), b.reshape(1, D).astype(F32))


def _ffn_up_kernel(x_ref, wg_ref, wu_ref, o_ref):
    x = x_ref[...]
    g = jnp.dot(x, wg_ref[...], preferred_element_type=F32)
    u = jnp.dot(x, wu_ref[...], preferred_element_type=F32)
    o_ref[...] = ((g / (1.0 + jnp.exp(-g))) * u).astype(o_ref.dtype)


def _ffn_up(x, wg, wu, *, bm=1024, bn=512):
    T, D = x.shape
    Fh = wg.shape[1]
    assert T % bm == 0 and Fh % bn == 0
    return pl.pallas_call(
        _ffn_up_kernel,
        grid=(T // bm, Fh // bn),
        in_specs=[pl.BlockSpec((bm, D), lambda i, j: (i, 0)),
                  pl.BlockSpec((D, bn), lambda i, j: (0, j)),
                  pl.BlockSpec((D, bn), lambda i, j: (0, j))],
        out_specs=pl.BlockSpec((bm, bn), lambda i, j: (i, j)),
        out_shape=jax.ShapeDtypeStruct((T, Fh), BF16),
        compiler_params=_cparams(("parallel", "parallel")),
        name="ffn_up",
    )(x, wg, wu)


def _flash_body(q, k_ref, v_ref, *, tk):
    M = q.shape[0]
    S, Dv = v_ref.shape

    def body(j, carry):
        m, l, acc = carry
        off = pl.multiple_of(j * tk, tk)
        ks = k_ref[pl.ds(off, tk), :]
        vs = v_ref[pl.ds(off, tk), :]
        s = lax.dot_general(q, ks, (((1,), (1,)), ((), ())), preferred_element_type=F32)
        m_new = jnp.maximum(m, jnp.max(s, axis=-1, keepdims=True))
        a = jnp.exp(m - m_new)
        p = jnp.exp(s - m_new)
        l = a * l + jnp.sum(p, axis=-1, keepdims=True)
        acc = a * acc + jnp.dot(p.astype(BF16), vs, preferred_element_type=F32)
        return m_new, l, acc

    init = (jnp.full((M, 1), -jnp.inf, F32), jnp.zeros((M, 1), F32), jnp.zeros((M, Dv), F32))
    _, l, acc = lax.fori_loop(0, S // tk, body, init)
    return acc / l


def _gqa_attn_kernel(q_ref, k_ref, v_ref, o_ref, *, G, tk):
    tq = q_ref.shape[0]
    q = jnp.concatenate([q_ref[:, g * LANE:(g + 1) * LANE] for g in range(G)], axis=0)
    o = _flash_body(q, k_ref, v_ref, tk=tk)
    for g in range(G):
        o_ref[:, g * LANE:(g + 1) * LANE] = o[g * tq:(g + 1) * tq].astype(o_ref.dtype)


def _gqa_attn(q, k, v, *, B, S, tq=128, tk=512):
    H, Hk = q.shape[1] // LANE, k.shape[1] // LANE
    G = H // Hk
    tk = min(tk, S)
    assert S % tq == 0 and S % tk == 0
    q3, k3, v3 = (t.reshape(B, S, t.shape[1]) for t in (q, k, v))
    kern = functools.partial(_gqa_attn_kernel, G=G, tk=tk)
    out = pl.pallas_call(
        kern,
        grid=(B, Hk, S // tq),
        in_specs=[pl.BlockSpec((None, tq, G * LANE), lambda b, h, i: (b, i, h)),
                  pl.BlockSpec((None, S, LANE), lambda b, h, i: (b, 0, h)),
                  pl.BlockSpec((None, S, LANE), lambda b, h, i: (b, 0, h))],
        out_specs=pl.BlockSpec((None, tq, G * LANE), lambda b, h, i: (b, i, h)),
        out_shape=jax.ShapeDtypeStruct((B, S, H * LANE), BF16),
        compiler_params=_cparams(("parallel", "parallel", "parallel")),
        name="gqa_attn",
    )(q3, k3, v3)
    return out.reshape(B * S, H * LANE)


def _mla_attn_kernel(q_ref, kn_ref, kr_ref, v_ref, o_ref, kc_ref, *, tk):
    @pl.when(pl.program_id(2) == 0)
    def _():
        kc_ref[:, :LANE] = kn_ref[...]
        kc_ref[:, LANE:] = kr_ref[...]

    o_ref[...] = _flash_body(q_ref[...], kc_ref, v_ref, tk=tk).astype(o_ref.dtype)


def _mla_attn(q, kv, kr, *, B, S, tq=512, tk=512):
    H = q.shape[1] // (2 * LANE)
    tq, tk = min(tq, S), min(tk, S)
    assert S % tq == 0 and S % tk == 0
    q3, kv3, kr3 = q.reshape(B, S, -1), kv.reshape(B, S, -1), kr.reshape(B, S, LANE)
    kern = functools.partial(_mla_attn_kernel, tk=tk)
    out = pl.pallas_call(
        kern,
        grid=(B, H, S // tq),
        in_specs=[pl.BlockSpec((None, tq, 2 * LANE), lambda b, h, i: (b, i, h)),
                  pl.BlockSpec((None, S, LANE), lambda b, h, i: (b, 0, 2 * h)),
                  pl.BlockSpec((None, S, LANE), lambda b, h, i: (b, 0, 0)),
                  pl.BlockSpec((None, S, LANE), lambda b, h, i: (b, 0, 2 * h + 1))],
        out_specs=pl.BlockSpec((None, tq, LANE), lambda b, h, i: (b, i, h)),
        out_shape=jax.ShapeDtypeStruct((B, S, H * LANE), BF16),
        scratch_shapes=[pltpu.VMEM((S, 2 * LANE), BF16)],
        compiler_params=_cparams(("arbitrary", "arbitrary", "arbitrary")),
        name="mla_attn",
    )(q3, kv3, kr3, kv3)
    return out.reshape(B * S, H * LANE)


def _diff_attn_kernel(slopes_ref, q_ref, k_ref, v_ref, lq1_ref, lk1_ref, lq2_ref, lk2_ref, sub_ref, o_ref,
                      *, tk, lam_init):
    tq = q_ref.shape[0]
    S, Dv = v_ref.shape
    slope = slopes_ref[pl.program_id(1)]
    q0 = (pl.program_id(2) * tq).astype(F32)
    rel = (lax.broadcasted_iota(jnp.int32, (tq, tk), 0)
           - lax.broadcasted_iota(jnp.int32, (tq, tk), 1)).astype(F32)
    qs = (q_ref[:, :DIFF_HD], q_ref[:, DIFF_HD:])

    def body(j, carry):
        off = pl.multiple_of(j * tk, tk)
        bias = slope * jnp.abs(rel + (q0 - off.astype(F32)))
        vs = v_ref[pl.ds(off, tk), :]
        new = []
        for c in range(2):
            m, l, acc = carry[c]
            ks = k_ref[pl.ds(off, tk), c * DIFF_HD:(c + 1) * DIFF_HD]
            s = lax.dot_general(qs[c], ks, (((1,), (1,)), ((), ())), preferred_element_type=F32) - bias
            m_new = jnp.maximum(m, jnp.max(s, axis=-1, keepdims=True))
            a = jnp.exp(m - m_new)
            p = jnp.exp(s - m_new)
            l = a * l + jnp.sum(p, axis=-1, keepdims=True)
            acc = a * acc + jnp.dot(p.astype(BF16), vs, preferred_element_type=F32)
            new.append((m_new, l, acc))
        return tuple(new)

    one = (jnp.full((tq, 1), -jnp.inf, F32), jnp.zeros((tq, 1), F32), jnp.zeros((tq, Dv), F32))
    (_, l0, a0), (_, l1, a1) = lax.fori_loop(0, S // tk, body, (one, one))
    lam = (jnp.exp(jnp.sum(lq1_ref[...] * lk1_ref[...], axis=-1, keepdims=True))
           - jnp.exp(jnp.sum(lq2_ref[...] * lk2_ref[...], axis=-1, keepdims=True)) + lam_init)
    o = a0 / l0 - lam * (a1 / l1)
    ms = jnp.mean(o * o, axis=-1, keepdims=True)
    o = o * lax.rsqrt(ms + RMS_EPS) * sub_ref[...] * (1.0 - lam_init)
    o_ref[...] = o.astype(o_ref.dtype)


def _diff_attn(q, kv, lam_vecs, subln, *, B, S, lam_init, tq=256, tk=512):
    W = 2 * DIFF_HD
    H = q.shape[1] // W
    tq, tk = min(tq, S), min(tk, S)
    assert S % tq == 0 and S % tk == 0
    slopes = jnp.exp2(-8.0 * jnp.arange(1, H + 1, dtype=F32) / H)
    q3, kv3 = q.reshape(B, S, -1), kv.reshape(B, S, -1)
    vec = pl.BlockSpec((1, DIFF_HD), lambda b, h, i: (0, 0))
    kern = functools.partial(_diff_attn_kernel, tk=tk, lam_init=lam_init)
    out = pl.pallas_call(
        kern,
        grid=(B, H, S // tq),
        in_specs=[pl.BlockSpec(memory_space=pltpu.SMEM),
                  pl.BlockSpec((None, tq, W), lambda b, h, i: (b, i, h)),
                  pl.BlockSpec((None, S, W), lambda b, h, i: (b, 0, h)),
                  pl.BlockSpec((None, S, W), lambda b, h, i: (b, 0, H + h)),
                  vec, vec, vec, vec,
                  pl.BlockSpec((1, W), lambda b, h, i: (0, 0))],
        out_specs=pl.BlockSpec((None, tq, W), lambda b, h, i: (b, i, h)),
        out_shape=jax.ShapeDtypeStruct((B, S, H * W), BF16),
        compiler_params=_cparams(("parallel", "parallel", "parallel")),
        name="diff_attn",
    )(slopes, q3, kv3, kv3, *[v.reshape(1, DIFF_HD).astype(F32) for v in lam_vecs],
      subln.reshape(1, W).astype(F32))
    return out.reshape(B * S, H * W)


def _na_rep_block(p, nblk):
    return jnp.where(p < 3, p, nblk - NA_WIN_BLOCKS + p)


def _na_bias_kernel(rpbT_ref, o_ref, *, nblk, R):
    nk = NA_WIN_BLOCKS * Q_BLOCK
    rows_per_blk = Q_BLOCK // GRID_W
    kh = min(NA_KH_MAX, R)
    j = _na_rep_block(pl.program_id(0), nblk)
    ws = jnp.clip(j - 2, 0, nblk - NA_WIN_BLOCKS)
    tab = rpbT_ref[...]
    n_dr = 2 * NA_KH_MAX - 1
    n_dc = 2 * NA_KW - 1

    w_shift = GRID_W.bit_length() - 1
    n1 = lax.broadcasted_iota(jnp.int32, (32, nk), 1)
    krow1 = rows_per_blk * ws + (n1 >> w_shift)
    for a in range(rows_per_blk):
        qrow = rows_per_blk * j + a
        drow1 = jnp.clip(krow1 - qrow + NA_KH_MAX - 1, 0, n_dr - 1)
        sel = jnp.zeros((32, nk), F32)
        for dr in range(n_dr):
            sel = jnp.where(drow1 == dr, tab[:, dr:dr + 1], sel)
        qc = lax.broadcasted_iota(jnp.int32, (GRID_W, nk), 0)
        n = lax.broadcasted_iota(jnp.int32, (GRID_W, nk), 1)
        kc = n & (GRID_W - 1)
        krow = rows_per_blk * ws + (n >> w_shift)
        rs = jnp.clip(qrow - kh // 2, 0, R - kh)
        cs = jnp.clip(qc - NA_KW // 2, 0, GRID_W - NA_KW)
        ok = (kc >= cs) & (kc < cs + NA_KW) & (krow >= rs) & (krow < rs + kh)
        dcol = jnp.clip(kc - qc + NA_KW - 1, 0, n_dc - 1)
        bias = jnp.zeros((GRID_W, nk), F32)
        for dc in range(n_dc):
            bias = jnp.where(dcol == dc, sel[dc:dc + 1, :], bias)
        o_ref[a * GRID_W:(a + 1) * GRID_W, :] = jnp.where(ok, bias, NEG_INF)


def _na_bias(rpb, *, S):
    H = rpb.shape[0]
    nblk, R = S // Q_BLOCK, S // GRID_W
    nk = NA_WIN_BLOCKS * Q_BLOCK
    rpbT = jnp.transpose(rpb.astype(F32), (0, 2, 1))
    rpbT = jnp.pad(rpbT, ((0, 0), (0, 32 - rpbT.shape[1]), (0, LANE - rpbT.shape[2])))
    kern = functools.partial(_na_bias_kernel, nblk=nblk, R=R)
    return pl.pallas_call(
        kern,
        grid=(NA_WIN_BLOCKS, H),
        in_specs=[pl.BlockSpec((None, 32, LANE), lambda p, h: (h, 0, 0))],
        out_specs=pl.BlockSpec((None, None, Q_BLOCK, nk), lambda p, h: (p, h, 0, 0)),
        out_shape=jax.ShapeDtypeStruct((NA_WIN_BLOCKS, H, Q_BLOCK, nk), F32),
        compiler_params=_cparams(("parallel", "parallel")),
        name="na_bias",
    )(rpbT)


def _na_attn_kernel(q_ref, *refs, H):
    k_refs = refs[:NA_WIN_BLOCKS]
    v_refs = refs[NA_WIN_BLOCKS:2 * NA_WIN_BLOCKS]
    bias_ref, o_ref = refs[2 * NA_WIN_BLOCKS], refs[2 * NA_WIN_BLOCKS + 1]
    for h in range(H):
        hs = slice(h * LANE, (h + 1) * LANE)
        k = jnp.concatenate([r[:, hs] for r in k_refs], axis=0)
        v = jnp.concatenate([r[:, hs] for r in v_refs], axis=0)
        s = lax.dot_general(q_ref[:, hs], k, (((1,), (1,)), ((), ())), preferred_element_type=F32)
        s = s + bias_ref[h]
        m = jnp.max(s, axis=-1, keepdims=True)
        p = jnp.exp(s - m)
        l = jnp.sum(p, axis=-1, keepdims=True)
        o = jnp.dot(p.astype(BF16), v, preferred_element_type=F32) / l
        o_ref[:, hs] = o.astype(o_ref.dtype)


def _na_attn(q, kv, bias, *, B, S):
    D = q.shape[1]
    H = D // LANE
    nblk = S // Q_BLOCK
    assert S % Q_BLOCK == 0 and nblk >= NA_WIN_BLOCKS
    q3, kv3 = q.reshape(B, S, D), kv.reshape(B, S, 2 * D)

    def win(j):
        return jnp.clip(j - 2, 0, nblk - NA_WIN_BLOCKS)

    def pattern(j):
        return jnp.where(j < 2, j, jnp.where(j <= nblk - 3, 2, j - (nblk - NA_WIN_BLOCKS)))

    k_specs = [pl.BlockSpec((None, Q_BLOCK, D), lambda b, j, t=t: (b, win(j) + t, 0))
               for t in range(NA_WIN_BLOCKS)]
    v_specs = [pl.BlockSpec((None, Q_BLOCK, D), lambda b, j, t=t: (b, win(j) + t, 1))
               for t in range(NA_WIN_BLOCKS)]
    nk = NA_WIN_BLOCKS * Q_BLOCK
    out = pl.pallas_call(
        functools.partial(_na_attn_kernel, H=H),
        grid=(B, nblk),
        in_specs=[pl.BlockSpec((None, Q_BLOCK, D), lambda b, j: (b, j, 0))] + k_specs + v_specs
                 + [pl.BlockSpec((None, H, Q_BLOCK, nk), lambda b, j: (pattern(j), 0, 0, 0))],
        out_specs=pl.BlockSpec((None, Q_BLOCK, D), lambda b, j: (b, j, 0)),
        out_shape=jax.ShapeDtypeStruct((B, S, D), BF16),
        compiler_params=_cparams(("parallel", "parallel")),
        name="na_attn",
    )(q3, *([kv3] * (2 * NA_WIN_BLOCKS)), bias)
    return out.reshape(B * S, D)


def _rope_cos_sin(pos, dim):
    inv = 1.0 / (ROPE_THETA ** (jnp.arange(0, dim, 2, dtype=F32) / dim))
    ang = pos.astype(F32)[:, None] * inv[None, :]
    return jnp.cos(ang), jnp.sin(ang)


def _rope_tables(pairs, S):
    z = jnp.zeros((S, 32), F32)
    c_parts, s1_parts, s2_parts = [], [], []
    for cos, sin in pairs:
        c_parts += [cos, cos]
        s1_parts += [-sin, z]
        s2_parts += [z, sin]
    pad = LANE - 64 * len(pairs)
    if pad:
        zp = jnp.zeros((S, pad), F32)
        c_parts.append(zp), s1_parts.append(zp), s2_parts.append(zp)
    return tuple(jnp.concatenate(p, axis=1) for p in (c_parts, s1_parts, s2_parts))


def _mla_mixer(h, hb, B, S, w_in, q_norm, w_q_up, kv_norm, w_kv_up):
    QR, KVR = q_norm.shape[0], kv_norm.shape[0]
    H = w_q_up.shape[1] // (MLA_NOPE + MLA_ROPE)
    tabs = _rope_tables([_rope_cos_sin(jnp.arange(S), MLA_ROPE)], S)
    w_in = w_in.astype(BF16)
    w_kr = jnp.pad(w_in[:, QR + KVR:], ((0, 0), (0, LANE - MLA_ROPE)))
    cq = _proj(hb, w_in[:, :QR], seq=S, norm="full", gain=q_norm)
    ckv = _proj(hb, w_in[:, QR:QR + KVR], seq=S, norm="full", gain=kv_norm)
    kr = _proj(hb, w_kr, seq=S, rope="all", tabs=tabs)
    wq = w_q_up.astype(BF16).reshape(QR, H, MLA_NOPE + MLA_ROPE)
    wq = jnp.pad(wq, ((0, 0), (0, 0), (0, 2 * LANE - MLA_NOPE - MLA_ROPE))).reshape(QR, H * 2 * LANE)
    q = _proj(cq, wq, seq=S, rope="odd", tabs=tabs, scale=(MLA_NOPE + MLA_ROPE) ** -0.5)
    kv = _proj(ckv, w_kv_up.astype(BF16), seq=S)
    return _mla_attn(q, kv, kr, B=B, S=S)


def _gqa_mixer(h, hb, B, S, w_qkv, q_norm, k_norm):
    HD = q_norm.shape[0]
    Hk = GQA_KV_HEADS
    nq = w_qkv.shape[1] - 2 * Hk * HD
    half = HD // 2
    pos = jnp.arange(S)
    tabs = _rope_tables([_rope_cos_sin(pos // GRID_W, half), _rope_cos_sin(pos % GRID_W, half)], S)
    w = w_qkv.astype(BF16)
    q = _proj(hb, w[:, :nq], seq=S, norm="group", gain=q_norm, rope="all", tabs=tabs, scale=HD ** -0.5)
    k = _proj(hb, w[:, nq:nq + Hk * HD], seq=S, norm="group", gain=k_norm, rope="all", tabs=tabs)
    v = _proj(hb, w[:, nq + Hk * HD:], seq=S)
    return _gqa_attn(q, k, v, B=B, S=S)


def _na_mixer(h, hb, B, S, w_qkv, rpb):
    D = w_qkv.shape[1] // 3
    w = w_qkv.astype(BF16)
    q = _proj(hb, w[:, :D], seq=S, scale=LANE ** -0.5)
    kv = _proj(hb, w[:, D:], seq=S)
    return _na_attn(q, kv, _na_bias(rpb, S=S), B=B, S=S)


def _diff_mixer(h, hb, B, S, w_qkv, lam_vecs, subln, lam_init):
    D = w_qkv.shape[1] // 3
    w = w_qkv.astype(BF16)
    q = _proj(hb, w[:, :D], seq=S, scale=DIFF_HD ** -0.5)
    kv = _proj(hb, w[:, D:], seq=S)
    return _diff_attn(q, kv, lam_vecs, subln, B=B, S=S, lam_init=lam_init)


def kernel(x, mla_w_in, mla_q_norm, mla_w_q_up, mla_kv_norm, mla_w_kv_up, mla_w_o, gqa_w_qkv, gqa_q_norm, gqa_k_norm, gqa_w_o, na_w_qkv, na_rpb, na_w_o, diff_w_qkv, diff_lambda_q1, diff_lambda_k1, diff_lambda_q2, diff_lambda_k2, diff_subln, diff_w_o, ffn_w_gate, ffn_w_up, ffn_w_down, ln_mix_g, ln_mix_b, ln_ffn_g, ln_ffn_b):
    B, S, D = x.shape
    depth = ffn_w_gate.shape[0]
    alpha = (2.0 * depth) ** 0.25
    h = x.reshape(B * S, D)
    hb = h
    for i in range(depth):
        m = i % 4
        if m == 0:
            o = _mla_mixer(h, hb, B, S, mla_w_in, mla_q_norm, mla_w_q_up, mla_kv_norm, mla_w_kv_up)
            w_o = mla_w_o
        elif m == 1:
            o = _gqa_mixer(h, hb, B, S, gqa_w_qkv, gqa_q_norm, gqa_k_norm)
            w_o = gqa_w_o
        elif m == 2:
            o = _na_mixer(h, hb, B, S, na_w_qkv, na_rpb)
            w_o = na_w_o
        else:
            lam_init = 0.8 - 0.6 * math.exp(-0.3 * i)
            o = _diff_mixer(h, hb, B, S, diff_w_qkv,
                            (diff_lambda_q1, diff_lambda_k1, diff_lambda_q2, diff_lambda_k2),
                            diff_subln, lam_init)
            w_o = diff_w_o
        h, hb = _mm_ln(o, w_o.astype(BF16), h, ln_mix_g[i], ln_mix_b[i], alpha=alpha)
        hid = _ffn_up(hb, ffn_w_gate[i].astype(BF16), ffn_w_up[i].astype(BF16))
        h, hb = _mm_ln(hid, ffn_w_down[i].astype(BF16), h, ln_ffn_g[i], ln_ffn_b[i], alpha=alpha)
    return h.reshape(B, S, D)
```

```python
import functools
import math

import jax
import jax.numpy as jnp
from jax import lax
from jax.experimental import pallas as pl
from jax.experimental.pallas import tpu as pltpu

F32 = jnp.float32
BF16 = jnp.bfloat16

LANE = 128
VMEM_LIMIT_BYTES = 56 * 1024 * 1024

GRID_W = 64
Q_BLOCK = 128
ROPE_THETA = 10000.0
NEG_INF = -1e30
MLA_NOPE, MLA_ROPE, MLA_V = 128, 64, 128
GQA_KV_HEADS = 4
NA_KH_MAX, NA_KW = 8, 16
NA_WIN_BLOCKS = 5
DIFF_HD = 128
LN_EPS = 1e-5
RMS_EPS = 1e-6
LOG2E = math.log2(math.e)


def _cparams(sem):
    return pltpu.CompilerParams(dimension_semantics=sem, vmem_limit_bytes=VMEM_LIMIT_BYTES)


def _proj_kernel(*refs, n_groups, norm, rope, scale):
    it = iter(refs)
    x_ref, w_ref = next(it), next(it)
    g_ref = next(it) if norm is not None else None
    tabs = (next(it), next(it), next(it)) if rope is not None else None
    o_ref = next(it)
    acc = jnp.dot(x_ref[...].astype(BF16), w_ref[...], preferred_element_type=F32)
    if norm == "full":
        ms = jnp.mean(acc * acc, axis=-1, keepdims=True)
        acc = acc * lax.rsqrt(ms + RMS_EPS) * g_ref[...]
    for g in range(n_groups):
        a = acc[:, g * LANE:(g + 1) * LANE]
        if norm == "group":
            ms = jnp.mean(a * a, axis=-1, keepdims=True)
            a = a * lax.rsqrt(ms + RMS_EPS) * g_ref[...]
        if rope == "all" or (rope == "odd" and g % 2 == 1):
            c, s1, s2 = tabs[0][...], tabs[1][...], tabs[2][...]
            a = a * c + pltpu.roll(a, 96, 1) * s1 + pltpu.roll(a, 32, 1) * s2
        if scale != 1.0:
            a = a * scale
        o_ref[:, g * LANE:(g + 1) * LANE] = a.astype(o_ref.dtype)


def _proj(x, w, *, seq, norm=None, gain=None, rope=None, tabs=None, scale=1.0, bm=1024, bn=None):
    T, K = x.shape
    N = w.shape[1]
    if bn is None:
        bn = N if N <= 1024 else 1024
    bm = min(bm, seq)
    assert T % bm == 0 and seq % bm == 0 and N % bn == 0 and bn % LANE == 0
    if norm == "full":
        assert bn == N
    if rope == "odd":
        assert (bn // LANE) % 2 == 0
    in_specs = [pl.BlockSpec((bm, K), lambda i, j: (i, 0)),
                pl.BlockSpec((K, bn), lambda i, j: (0, j))]
    args = [x, w]
    if norm == "full":
        in_specs.append(pl.BlockSpec((1, bn), lambda i, j: (0, 0)))
        args.append(gain.reshape(1, N).astype(F32))
    elif norm == "group":
        in_specs.append(pl.BlockSpec((1, LANE), lambda i, j: (0, 0)))
        args.append(gain.reshape(1, LANE).astype(F32))
    if rope is not None:
        nseq = seq // bm
        for t in tabs:
            in_specs.append(pl.BlockSpec((bm, LANE), lambda i, j: (i % nseq, 0)))
            args.append(t)
    kern = functools.partial(_proj_kernel, n_groups=bn // LANE, norm=norm, rope=rope, scale=scale)
    return pl.pallas_call(
        kern,
        grid=(T // bm, N // bn),
        in_specs=in_specs,
        out_specs=pl.BlockSpec((bm, bn), lambda i, j: (i, j)),
        out_shape=jax.ShapeDtypeStruct((T, N), BF16),
        compiler_params=_cparams(("parallel", "parallel")),
        name="proj",
    )(*args)


def _mm_ln_kernel(a_ref, w_ref, h_ref, g_ref, b_ref, of_ref, ob_ref, acc_ref, *, alpha, nk):
    k = pl.program_id(1)
    part = jnp.dot(a_ref[...], w_ref[...], preferred_element_type=F32)

    @pl.when(k == 0)
    def _():
        acc_ref[...] = part

    @pl.when(k > 0)
    def _():
        acc_ref[...] += part

    @pl.when(k == nk - 1)
    def _():
        y = alpha * h_ref[...] + acc_ref[...]
        mu = jnp.mean(y, axis=-1, keepdims=True)
        d = y - mu
        var = jnp.mean(d * d, axis=-1, keepdims=True)
        out = d * lax.rsqrt(var + LN_EPS) * g_ref[...] + b_ref[...]
        of_ref[...] = out
        ob_ref[...] = out.astype(BF16)


def _pick_bk(K):
    for bk in (1024, 1408, 512, 256, 128):
        if K % bk == 0:
            return bk
    raise ValueError(K)


def _mm_ln(a, w, h, g, b, *, alpha, bm=512):
    T, K = a.shape
    D = w.shape[1]
    bk = _pick_bk(K)
    nk = K // bk
    assert T % bm == 0
    kern = functools.partial(_mm_ln_kernel, alpha=alpha, nk=nk)
    return pl.pallas_call(
        kern,
        grid=(T // bm, nk),
        in_specs=[pl.BlockSpec((bm, bk), lambda i, k: (i, k)),
                  pl.BlockSpec((bk, D), lambda i, k: (k, 0)),
                  pl.BlockSpec((bm, D), lambda i, k: (i, 0)),
                  pl.BlockSpec((1, D), lambda i, k: (0, 0)),
                  pl.BlockSpec((1, D), lambda i, k: (0, 0))],
        out_specs=[pl.BlockSpec((bm, D), lambda i, k: (i, 0)),
                   pl.BlockSpec((bm, D), lambda i, k: (i, 0))],
        out_shape=[jax.ShapeDtypeStruct((T, D), F32), jax.ShapeDtypeStruct((T, D), BF16)],
        scratch_shapes=[pltpu.VMEM((bm, D), F32)],
        compiler_params=_cparams(("parallel", "arbitrary")),
        name="mm_ln",
    )(a, w, h, g.reshape(1, D).astype(F32), b.reshape(1, D).astype(F32))


def _ffn_up_kernel(x_ref, wg_ref, wu_ref, o_ref):
    x = x_ref[...]
    g = jnp.dot(x, wg_ref[...], preferred_element_type=F32)
    u = jnp.dot(x, wu_ref[...], preferred_element_type=F32)
    o_ref[...] = ((g / (1.0 + jnp.exp(-g))) * u).astype(o_ref.dtype)


def _ffn_up(x, wg, wu, *, bm=1024, bn=512):
    T, D = x.shape
    Fh = wg.shape[1]
    assert T % bm == 0 and Fh % bn == 0
    return pl.pallas_call(
        _ffn_up_kernel,
        grid=(T // bm, Fh // bn),
        in_specs=[pl.BlockSpec((bm, D), lambda i, j: (i, 0)),
                  pl.BlockSpec((D, bn), lambda i, j: (0, j)),
                  pl.BlockSpec((D, bn), lambda i, j: (0, j))],
        out_specs=pl.BlockSpec((bm, bn), lambda i, j: (i, j)),
        out_shape=jax.ShapeDtypeStruct((T, Fh), BF16),
        compiler_params=_cparams(("parallel", "parallel")),
        name="ffn_up",
    )(x, wg, wu)


def _flash_loop(scores, v_ref, s_refs, *, M, tk):
    S, Dv = v_ref.shape
    n = S // tk
    assert n % 2 == 0

    def absorb(c, s_ref, mx, carry):
        m, l, acc = carry
        off = pl.multiple_of(c * tk, tk)
        m_new = jnp.maximum(m, mx)
        a = jnp.exp2(m - m_new)
        p = jnp.exp2(s_ref[...] - m_new)
        l = a * l + jnp.sum(p, axis=0, keepdims=True)
        pv = lax.dot_general(v_ref[pl.ds(off, tk), :], p.astype(BF16), (((0,), (0,)), ((), ())),
                             preferred_element_type=F32)
        return m_new, l, a * acc + pv

    def body(jj, carry):
        mx0, state = carry
        c0 = 2 * jj
        mx1 = scores(c0 + 1, s_refs[1])
        state = absorb(c0, s_refs[0], mx0, state)
        mx0 = scores(jnp.minimum(c0 + 2, n - 1), s_refs[0])
        state = absorb(c0 + 1, s_refs[1], mx1, state)
        return mx0, state

    init = (jnp.full((1, M), -jnp.inf, F32), jnp.zeros((1, M), F32), jnp.zeros((Dv, M), F32))
    _, (_, l, acc) = lax.fori_loop(0, n // 2, body, (scores(0, s_refs[0]), init))
    return l, acc


def _flash_body(q, k_ref, v_ref, s_refs, *, tk):
    qt = q.astype(F32).T.astype(BF16)

    def scores(c, s_ref):
        off = pl.multiple_of(c * tk, tk)
        st = jnp.dot(k_ref[pl.ds(off, tk), :], qt, preferred_element_type=F32)
        s_ref[...] = st
        return jnp.max(st, axis=0, keepdims=True)

    l, acc = _flash_loop(scores, v_ref, s_refs, M=q.shape[0], tk=tk)
    return (acc / l).T


def _gqa_attn_kernel(q_ref, k_ref, v_ref, o_ref, s0_ref, s1_ref, *, G, tk):
    tq = q_ref.shape[0]
    q = jnp.concatenate([q_ref[:, g * LANE:(g + 1) * LANE] for g in range(G)], axis=0)
    o = _flash_body(q, k_ref, v_ref, (s0_ref, s1_ref), tk=tk)
    for g in range(G):
        o_ref[:, g * LANE:(g + 1) * LANE] = o[g * tq:(g + 1) * tq].astype(o_ref.dtype)


def _gqa_attn(q, k, v, *, B, S, tq=128, tk=512):
    H, Hk = q.shape[1] // LANE, k.shape[1] // LANE
    G = H // Hk
    tk = min(tk, S)
    assert S % tq == 0 and S % tk == 0
    q3, k3, v3 = (t.reshape(B, S, t.shape[1]) for t in (q, k, v))
    kern = functools.partial(_gqa_attn_kernel, G=G, tk=tk)
    out = pl.pallas_call(
        kern,
        grid=(B, Hk, S // tq),
        in_specs=[pl.BlockSpec((None, tq, G * LANE), lambda b, h, i: (b, i, h)),
                  pl.BlockSpec((None, S, LANE), lambda b, h, i: (b, 0, h)),
                  pl.BlockSpec((None, S, LANE), lambda b, h, i: (b, 0, h))],
        out_specs=pl.BlockSpec((None, tq, G * LANE), lambda b, h, i: (b, i, h)),
        out_shape=jax.ShapeDtypeStruct((B, S, H * LANE), BF16),
        scratch_shapes=[pltpu.VMEM((tk, G * tq), F32)] * 2,
        compiler_params=_cparams(("parallel", "parallel", "parallel")),
        name="gqa_attn",
    )(q3, k3, v3)
    return out.reshape(B * S, H * LANE)


def _mla_attn_kernel(q_ref, kn_ref, kr_ref, v_ref, o_ref, kc_ref, s0_ref, s1_ref, *, tk):
    @pl.when(pl.program_id(2) == 0)
    def _():
        kc_ref[:, :LANE] = kn_ref[...]
        kc_ref[:, LANE:] = kr_ref[...]

    o_ref[...] = _flash_body(q_ref[...], kc_ref, v_ref, (s0_ref, s1_ref), tk=tk).astype(o_ref.dtype)


def _mla_attn(q, kv, kr, *, B, S, tq=512, tk=512):
    H = q.shape[1] // (2 * LANE)
    tq, tk = min(tq, S), min(tk, S)
    assert S % tq == 0 and S % tk == 0
    q3, kv3, kr3 = q.reshape(B, S, -1), kv.reshape(B, S, -1), kr.reshape(B, S, LANE)
    kern = functools.partial(_mla_attn_kernel, tk=tk)
    out = pl.pallas_call(
        kern,
        grid=(B, H, S // tq),
        in_specs=[pl.BlockSpec((None, tq, 2 * LANE), lambda b, h, i: (b, i, h)),
                  pl.BlockSpec((None, S, LANE), lambda b, h, i: (b, 0, 2 * h)),
                  pl.BlockSpec((None, S, LANE), lambda b, h, i: (b, 0, 0)),
                  pl.BlockSpec((None, S, LANE), lambda b, h, i: (b, 0, 2 * h + 1))],
        out_specs=pl.BlockSpec((None, tq, LANE), lambda b, h, i: (b, i, h)),
        out_shape=jax.ShapeDtypeStruct((B, S, H * LANE), BF16),
        scratch_shapes=[pltpu.VMEM((S, 2 * LANE), BF16)] + [pltpu.VMEM((tk, tq), F32)] * 2,
        compiler_params=_cparams(("arbitrary", "arbitrary", "arbitrary")),
        name="mla_attn",
    )(q3, kv3, kr3, kv3)
    return out.reshape(B * S, H * LANE)


def _diff_attn_kernel(slopes_ref, q_ref, k_ref, v_ref, lq1_ref, lk1_ref, lq2_ref, lk2_ref, sub_ref, o_ref,
                      s0_ref, s1_ref, *, tk, lam_init):
    tq = q_ref.shape[0]
    slope = slopes_ref[pl.program_id(1)]
    q0 = pl.program_id(2) * tq
    arel = slope * (lax.broadcasted_iota(jnp.int32, (tk, tq), 0)
                    - lax.broadcasted_iota(jnp.int32, (tk, tq), 1)).astype(F32)
    qts = [q_ref[:, c * DIFF_HD:(c + 1) * DIFF_HD].astype(F32).T.astype(BF16) for c in range(2)]

    def scores(c, s_ref):
        off = pl.multiple_of(c * tk, tk)
        bias = jnp.abs(arel + slope * (off - q0).astype(F32))
        mxs = []
        for comp in range(2):
            ks = k_ref[pl.ds(off, tk), comp * DIFF_HD:(comp + 1) * DIFF_HD]
            st = jnp.dot(ks, qts[comp], preferred_element_type=F32) - bias
            s_ref[:, comp * tq:(comp + 1) * tq] = st
            mxs.append(jnp.max(st, axis=0, keepdims=True))
        return jnp.concatenate(mxs, axis=1)

    l, acc = _flash_loop(scores, v_ref, (s0_ref, s1_ref), M=2 * tq, tk=tk)
    on = acc / l
    lam = (jnp.exp(jnp.sum(lq1_ref[...] * lk1_ref[...], axis=-1, keepdims=True))
           - jnp.exp(jnp.sum(lq2_ref[...] * lk2_ref[...], axis=-1, keepdims=True)) + lam_init)
    o = (on[:, :tq] - lam * on[:, tq:]).T
    ms = jnp.mean(o * o, axis=-1, keepdims=True)
    o = o * lax.rsqrt(ms + RMS_EPS) * sub_ref[...] * (1.0 - lam_init)
    o_ref[...] = o.astype(o_ref.dtype)


def _diff_attn(q, kv, lam_vecs, subln, *, B, S, lam_init, tq=256, tk=512):
    W = 2 * DIFF_HD
    H = q.shape[1] // W
    tq, tk = min(tq, S), min(tk, S)
    assert S % tq == 0 and S % tk == 0
    slopes = LOG2E * jnp.exp2(-8.0 * jnp.arange(1, H + 1, dtype=F32) / H)
    q3, kv3 = q.reshape(B, S, -1), kv.reshape(B, S, -1)
    vec = pl.BlockSpec((1, DIFF_HD), lambda b, h, i: (0, 0))
    kern = functools.partial(_diff_attn_kernel, tk=tk, lam_init=lam_init)
    out = pl.pallas_call(
        kern,
        grid=(B, H, S // tq),
        in_specs=[pl.BlockSpec(memory_space=pltpu.SMEM),
                  pl.BlockSpec((None, tq, W), lambda b, h, i: (b, i, h)),
                  pl.BlockSpec((None, S, W), lambda b, h, i: (b, 0, h)),
                  pl.BlockSpec((None, S, W), lambda b, h, i: (b, 0, H + h)),
                  vec, vec, vec, vec,
                  pl.BlockSpec((1, W), lambda b, h, i: (0, 0))],
        out_specs=pl.BlockSpec((None, tq, W), lambda b, h, i: (b, i, h)),
        out_shape=jax.ShapeDtypeStruct((B, S, H * W), BF16),
        scratch_shapes=[pltpu.VMEM((tk, 2 * tq), F32)] * 2,
        compiler_params=_cparams(("parallel", "parallel", "parallel")),
        name="diff_attn",
    )(slopes, q3, kv3, kv3, *[v.reshape(1, DIFF_HD).astype(F32) for v in lam_vecs],
      subln.reshape(1, W).astype(F32))
    return out.reshape(B * S, H * W)


def _na_rep_block(p, nblk):
    return jnp.where(p < 3, p, nblk - NA_WIN_BLOCKS + p)


def _na_bias_kernel(rpbT_ref, o_ref, *, nblk, R):
    nk = NA_WIN_BLOCKS * Q_BLOCK
    rows_per_blk = Q_BLOCK // GRID_W
    kh = min(NA_KH_MAX, R)
    j = _na_rep_block(pl.program_id(0), nblk)
    ws = jnp.clip(j - 2, 0, nblk - NA_WIN_BLOCKS)
    tab = rpbT_ref[...]
    n_dr = 2 * NA_KH_MAX - 1
    n_dc = 2 * NA_KW - 1

    w_shift = GRID_W.bit_length() - 1
    n1 = lax.broadcasted_iota(jnp.int32, (32, nk), 1)
    krow1 = rows_per_blk * ws + (n1 >> w_shift)
    for a in range(rows_per_blk):
        qrow = rows_per_blk * j + a
        drow1 = jnp.clip(krow1 - qrow + NA_KH_MAX - 1, 0, n_dr - 1)
        sel = jnp.zeros((32, nk), F32)
        for dr in range(n_dr):
            sel = jnp.where(drow1 == dr, tab[:, dr:dr + 1], sel)
        qc = lax.broadcasted_iota(jnp.int32, (GRID_W, nk), 0)
        n = lax.broadcasted_iota(jnp.int32, (GRID_W, nk), 1)
        kc = n & (GRID_W - 1)
        krow = rows_per_blk * ws + (n >> w_shift)
        rs = jnp.clip(qrow - kh // 2, 0, R - kh)
        cs = jnp.clip(qc - NA_KW // 2, 0, GRID_W - NA_KW)
        ok = (kc >= cs) & (kc < cs + NA_KW) & (krow >= rs) & (krow < rs + kh)
        dcol = jnp.clip(kc - qc + NA_KW - 1, 0, n_dc - 1)
        bias = jnp.zeros((GRID_W, nk), F32)
        for dc in range(n_dc):
            bias = jnp.where(dcol == dc, sel[dc:dc + 1, :], bias)
        o_ref[a * GRID_W:(a + 1) * GRID_W, :] = jnp.where(ok, bias, NEG_INF)


def _na_bias(rpb, *, S):
    H = rpb.shape[0]
    nblk, R = S // Q_BLOCK, S // GRID_W
    nk = NA_WIN_BLOCKS * Q_BLOCK
    rpbT = jnp.transpose(rpb.astype(F32), (0, 2, 1))
    rpbT = jnp.pad(rpbT, ((0, 0), (0, 32 - rpbT.shape[1]), (0, LANE - rpbT.shape[2])))
    kern = functools.partial(_na_bias_kernel, nblk=nblk, R=R)
    return pl.pallas_call(
        kern,
        grid=(NA_WIN_BLOCKS, H),
        in_specs=[pl.BlockSpec((None, 32, LANE), lambda p, h: (h, 0, 0))],
        out_specs=pl.BlockSpec((None, None, Q_BLOCK, nk), lambda p, h: (p, h, 0, 0)),
        out_shape=jax.ShapeDtypeStruct((NA_WIN_BLOCKS, H, Q_BLOCK, nk), F32),
        compiler_params=_cparams(("parallel", "parallel")),
        name="na_bias",
    )(rpbT)


def _na_attn_kernel(q_ref, *refs, H):
    k_refs = refs[:NA_WIN_BLOCKS]
    v_refs = refs[NA_WIN_BLOCKS:2 * NA_WIN_BLOCKS]
    bias_ref, o_ref = refs[2 * NA_WIN_BLOCKS], refs[2 * NA_WIN_BLOCKS + 1]
    for h in range(H):
        hs = slice(h * LANE, (h + 1) * LANE)
        k = jnp.concatenate([r[:, hs] for r in k_refs], axis=0)
        v = jnp.concatenate([r[:, hs] for r in v_refs], axis=0)
        s = lax.dot_general(q_ref[:, hs], k, (((1,), (1,)), ((), ())), preferred_element_type=F32)
        s = s + bias_ref[h]
        m = jnp.max(s, axis=-1, keepdims=True)
        p = jnp.exp(s - m)
        l = jnp.sum(p, axis=-1, keepdims=True)
        o = jnp.dot(p.astype(BF16), v, preferred_element_type=F32) / l
        o_ref[:, hs] = o.astype(o_ref.dtype)


def _na_attn(q, kv, bias, *, B, S):
    D = q.shape[1]
    H = D // LANE
    nblk = S // Q_BLOCK
    assert S % Q_BLOCK == 0 and nblk >= NA_WIN_BLOCKS
    q3, kv3 = q.reshape(B, S, D), kv.reshape(B, S, 2 * D)

    def win(j):
        return jnp.clip(j - 2, 0, nblk - NA_WIN_BLOCKS)

    def pattern(j):
        return jnp.where(j < 2, j, jnp.where(j <= nblk - 3, 2, j - (nblk - NA_WIN_BLOCKS)))

    k_specs = [pl.BlockSpec((None, Q_BLOCK, D), lambda b, j, t=t: (b, win(j) + t, 0))
               for t in range(NA_WIN_BLOCKS)]
    v_specs = [pl.BlockSpec((None, Q_BLOCK, D), lambda b, j, t=t: (b, win(j) + t, 1))
               for t in range(NA_WIN_BLOCKS)]
    nk = NA_WIN_BLOCKS * Q_BLOCK
    out = pl.pallas_call(
        functools.partial(_na_attn_kernel, H=H),
        grid=(B, nblk),
        in_specs=[pl.BlockSpec((None, Q_BLOCK, D), lambda b, j: (b, j, 0))] + k_specs + v_specs
                 + [pl.BlockSpec((None, H, Q_BLOCK, nk), lambda b, j: (pattern(j), 0, 0, 0))],
        out_specs=pl.BlockSpec((None, Q_BLOCK, D), lambda b, j: (b, j, 0)),
        out_shape=jax.ShapeDtypeStruct((B, S, D), BF16),
        compiler_params=_cparams(("parallel", "parallel")),
        name="na_attn",
    )(q3, *([kv3] * (2 * NA_WIN_BLOCKS)), bias)
    return out.reshape(B * S, D)


def _rope_cos_sin(pos, dim):
    inv = 1.0 / (ROPE_THETA ** (jnp.arange(0, dim, 2, dtype=F32) / dim))
    ang = pos.astype(F32)[:, None] * inv[None, :]
    return jnp.cos(ang), jnp.sin(ang)


def _rope_tables(pairs, S):
    z = jnp.zeros((S, 32), F32)
    c_parts, s1_parts, s2_parts = [], [], []
    for cos, sin in pairs:
        c_parts += [cos, cos]
        s1_parts += [-sin, z]
        s2_parts += [z, sin]
    pad = LANE - 64 * len(pairs)
    if pad:
        zp = jnp.zeros((S, pad), F32)
        c_parts.append(zp), s1_parts.append(zp), s2_parts.append(zp)
    return tuple(jnp.concatenate(p, axis=1) for p in (c_parts, s1_parts, s2_parts))


def _mla_mixer(h, hb, B, S, w_in, q_norm, w_q_up, kv_norm, w_kv_up):
    QR, KVR = q_norm.shape[0], kv_norm.shape[0]
    H = w_q_up.shape[1] // (MLA_NOPE + MLA_ROPE)
    tabs = _rope_tables([_rope_cos_sin(jnp.arange(S), MLA_ROPE)], S)
    w_in = w_in.astype(BF16)
    w_kr = jnp.pad(w_in[:, QR + KVR:], ((0, 0), (0, LANE - MLA_ROPE)))
    cq = _proj(hb, w_in[:, :QR], seq=S, norm="full", gain=q_norm)
    ckv = _proj(hb, w_in[:, QR:QR + KVR], seq=S, norm="full", gain=kv_norm)
    kr = _proj(hb, w_kr, seq=S, rope="all", tabs=tabs)
    wq = w_q_up.astype(BF16).reshape(QR, H, MLA_NOPE + MLA_ROPE)
    wq = jnp.pad(wq, ((0, 0), (0, 0), (0, 2 * LANE - MLA_NOPE - MLA_ROPE))).reshape(QR, H * 2 * LANE)
    q = _proj(cq, wq, seq=S, rope="odd", tabs=tabs, scale=LOG2E * (MLA_NOPE + MLA_ROPE) ** -0.5)
    kv = _proj(ckv, w_kv_up.astype(BF16), seq=S)
    return _mla_attn(q, kv, kr, B=B, S=S)


def _gqa_mixer(h, hb, B, S, w_qkv, q_norm, k_norm):
    HD = q_norm.shape[0]
    Hk = GQA_KV_HEADS
    nq = w_qkv.shape[1] - 2 * Hk * HD
    half = HD // 2
    pos = jnp.arange(S)
    tabs = _rope_tables([_rope_cos_sin(pos // GRID_W, half), _rope_cos_sin(pos % GRID_W, half)], S)
    w = w_qkv.astype(BF16)
    q = _proj(hb, w[:, :nq], seq=S, norm="group", gain=q_norm, rope="all", tabs=tabs, scale=LOG2E * HD ** -0.5)
    k = _proj(hb, w[:, nq:nq + Hk * HD], seq=S, norm="group", gain=k_norm, rope="all", tabs=tabs)
    v = _proj(hb, w[:, nq + Hk * HD:], seq=S)
    return _gqa_attn(q, k, v, B=B, S=S)


def _na_mixer(h, hb, B, S, w_qkv, rpb):
    D = w_qkv.shape[1] // 3
    w = w_qkv.astype(BF16)
    q = _proj(hb, w[:, :D], seq=S, scale=LANE ** -0.5)
    kv = _proj(hb, w[:, D:], seq=S)
    return _na_attn(q, kv, _na_bias(rpb, S=S), B=B, S=S)


def _diff_mixer(h, hb, B, S, w_qkv, lam_vecs, subln, lam_init):
    D = w_qkv.shape[1] // 3
    w = w_qkv.astype(BF16)
    q = _proj(hb, w[:, :D], seq=S, scale=LOG2E * DIFF_HD ** -0.5)
    kv = _proj(hb, w[:, D:], seq=S)
    return _diff_attn(q, kv, lam_vecs, subln, B=B, S=S, lam_init=lam_init)


def kernel(x, mla_w_in, mla_q_norm, mla_w_q_up, mla_kv_norm, mla_w_kv_up, mla_w_o, gqa_w_qkv, gqa_q_norm, gqa_k_norm, gqa_w_o, na_w_qkv, na_rpb, na_w_o, diff_w_qkv, diff_lambda_q1, diff_lambda_k1, diff_lambda_q2, diff_lambda_k2, diff_subln, diff_w_o, ffn_w_gate, ffn_w_up, ffn_w_down, ln_mix_g, ln_mix_b, ln_ffn_g, ln_ffn_b):
    B, S, D = x.shape
    depth = ffn_w_gate.shape[0]
    alpha = (2.0 * depth) ** 0.25
    h = x.reshape(B * S, D)
    hb = h
    for i in range(depth):
        m = i % 4
        if m == 0:
            o = _mla_mixer(h, hb, B, S, mla_w_in, mla_q_norm, mla_w_q_up, mla_kv_norm, mla_w_kv_up)
            w_o = mla_w_o
        elif m == 1:
            o = _gqa_mixer(h, hb, B, S, gqa_w_qkv, gqa_q_norm, gqa_k_norm)
            w_o = gqa_w_o
        elif m == 2:
            o = _na_mixer(h, hb, B, S, na_w_qkv, na_rpb)
            w_o = na_w_o
        else:
            lam_init = 0.8 - 0.6 * math.exp(-0.3 * i)
            o = _diff_mixer(h, hb, B, S, diff_w_qkv,
                            (diff_lambda_q1, diff_lambda_k1, diff_lambda_q2, diff_lambda_k2),
                            diff_subln, lam_init)
            w_o = diff_w_o
        h, hb = _mm_ln(o, w_o.astype(BF16), h, ln_mix_g[i], ln_mix_b[i], alpha=alpha)
        hid = _ffn_up(hb, ffn_w_gate[i].astype(BF16), ffn_w_up[i].astype(BF16))
        h, hb = _mm_ln(hid, ffn_w_down[i].astype(BF16), h, ln_ffn_g[i], ln_ffn_b[i], alpha=alpha)
    return h.reshape(B, S, D)
```

```python
import functools
import math

import jax
import jax.numpy as jnp
from jax import lax
from jax.experimental import pallas as pl
from jax.experimental.pallas import tpu as pltpu

F32 = jnp.float32
BF16 = jnp.bfloat16

LANE = 128
VMEM_LIMIT_BYTES = 56 * 1024 * 1024

GRID_W = 64
Q_BLOCK = 128
ROPE_THETA = 10000.0
NEG_INF = -1e30
MLA_NOPE, MLA_ROPE, MLA_V = 128, 64, 128
GQA_KV_HEADS = 4
NA_KH_MAX, NA_KW = 8, 16
NA_WIN_BLOCKS = 5
DIFF_HD = 128
LN_EPS = 1e-5
RMS_EPS = 1e-6
LOG2E = math.log2(math.e)
FLASH_BODY_KEYS = 16384
PROJ_ROW_BLOCK = 256
DIFF_BODY_KEYS = 8192


def _cparams(sem):
    return pltpu.CompilerParams(dimension_semantics=sem, vmem_limit_bytes=VMEM_LIMIT_BYTES)


def _proj_kernel(*refs, n_groups, norm, rope, scale):
    it = iter(refs)
    x_ref, w_ref = next(it), next(it)
    g_ref = next(it) if norm is not None else None
    tabs = (next(it), next(it), next(it)) if rope is not None else None
    o_ref = next(it)
    bm = x_ref.shape[0]
    rb = min(bm, PROJ_ROW_BLOCK)
    for r in range(bm // rb):
        rows = slice(r * rb, (r + 1) * rb)
        acc = jnp.dot(x_ref[rows, :].astype(BF16), w_ref[...], preferred_element_type=F32)
        if norm == "full":
            ms = jnp.mean(acc * acc, axis=-1, keepdims=True)
            acc = acc * lax.rsqrt(ms + RMS_EPS) * g_ref[...]
        for g in range(n_groups):
            a = acc[:, g * LANE:(g + 1) * LANE]
            if norm == "group":
                ms = jnp.mean(a * a, axis=-1, keepdims=True)
                a = a * lax.rsqrt(ms + RMS_EPS) * g_ref[...]
            if rope == "all" or (rope == "odd" and g % 2 == 1):
                c, s1, s2 = tabs[0][rows, :], tabs[1][rows, :], tabs[2][rows, :]
                a = a * c + pltpu.roll(a, 96, 1) * s1 + pltpu.roll(a, 32, 1) * s2
            if scale != 1.0:
                a = a * scale
            o_ref[rows, g * LANE:(g + 1) * LANE] = a.astype(o_ref.dtype)


def _proj(x, w, *, seq, norm=None, gain=None, rope=None, tabs=None, scale=1.0, bm=1024, bn=None):
    T, K = x.shape
    N = w.shape[1]
    if bn is None:
        bn = N if N <= 1024 else 1024
    bm = min(bm, seq)
    assert T % bm == 0 and seq % bm == 0 and N % bn == 0 and bn % LANE == 0
    if norm == "full":
        assert bn == N
    if rope == "odd":
        assert (bn // LANE) % 2 == 0
    in_specs = [pl.BlockSpec((bm, K), lambda i, j: (i, 0)),
                pl.BlockSpec((K, bn), lambda i, j: (0, j))]
    args = [x, w]
    if norm == "full":
        in_specs.append(pl.BlockSpec((1, bn), lambda i, j: (0, 0)))
        args.append(gain.reshape(1, N).astype(F32))
    elif norm == "group":
        in_specs.append(pl.BlockSpec((1, LANE), lambda i, j: (0, 0)))
        args.append(gain.reshape(1, LANE).astype(F32))
    if rope is not None:
        nseq = seq // bm
        for t in tabs:
            in_specs.append(pl.BlockSpec((bm, LANE), lambda i, j: (i % nseq, 0)))
            args.append(t)
    kern = functools.partial(_proj_kernel, n_groups=bn // LANE, norm=norm, rope=rope, scale=scale)
    return pl.pallas_call(
        kern,
        grid=(T // bm, N // bn),
        in_specs=in_specs,
        out_specs=pl.BlockSpec((bm, bn), lambda i, j: (i, j)),
        out_shape=jax.ShapeDtypeStruct((T, N), BF16),
        compiler_params=_cparams(("parallel", "parallel")),
        name="proj",
    )(*args)


def _mm_ln_kernel(a_ref, w_ref, h_ref, g_ref, b_ref, of_ref, ob_ref, *, alpha):
    y = alpha * h_ref[...] + jnp.dot(a_ref[...], w_ref[...], preferred_element_type=F32)
    mu = jnp.mean(y, axis=-1, keepdims=True)
    d = y - mu
    var = jnp.mean(d * d, axis=-1, keepdims=True)
    out = d * lax.rsqrt(var + LN_EPS) * g_ref[...] + b_ref[...]
    of_ref[...] = out
    ob_ref[...] = out.astype(BF16)


def _mm_ln(a, w, h, g, b, *, alpha):
    T, K = a.shape
    D = w.shape[1]
    row_bytes = 2 * (2 * K) + 2 * (4 * D) + 2 * (4 * D) + 2 * (2 * D) + 4 * D
    budget = VMEM_LIMIT_BYTES - 2 * K * D - (8 << 20)
    bm = 512
    while bm > 8 and (bm * row_bytes > budget or T % bm):
        bm //= 2
    assert T % bm == 0
    resident = pl.Buffered(1)
    return pl.pallas_call(
        functools.partial(_mm_ln_kernel, alpha=alpha),
        grid=(T // bm,),
        in_specs=[pl.BlockSpec((bm, K), lambda i: (i, 0)),
                  pl.BlockSpec((K, D), lambda i: (0, 0), pipeline_mode=resident),
                  pl.BlockSpec((bm, D), lambda i: (i, 0)),
                  pl.BlockSpec((1, D), lambda i: (0, 0)),
                  pl.BlockSpec((1, D), lambda i: (0, 0))],
        out_specs=[pl.BlockSpec((bm, D), lambda i: (i, 0)),
                   pl.BlockSpec((bm, D), lambda i: (i, 0))],
        out_shape=[jax.ShapeDtypeStruct((T, D), F32), jax.ShapeDtypeStruct((T, D), BF16)],
        compiler_params=_cparams(("parallel",)),
        name="mm_ln",
    )(a, w, h, g.reshape(1, D).astype(F32), b.reshape(1, D).astype(F32))


def _ffn_up_kernel(x_ref, wg_ref, wu_ref, o_ref):
    x = x_ref[...]
    g = jnp.dot(x, wg_ref[...], preferred_element_type=F32)
    u = jnp.dot(x, wu_ref[...], preferred_element_type=F32)
    o_ref[...] = ((g / (1.0 + jnp.exp(-g))) * u).astype(o_ref.dtype)


def _ffn_up(x, wg, wu, *, bm=1024, bn=512):
    T, D = x.shape
    Fh = wg.shape[1]
    assert T % bm == 0 and Fh % bn == 0
    return pl.pallas_call(
        _ffn_up_kernel,
        grid=(T // bm, Fh // bn),
        in_specs=[pl.BlockSpec((bm, D), lambda i, j: (i, 0)),
                  pl.BlockSpec((D, bn), lambda i, j: (0, j)),
                  pl.BlockSpec((D, bn), lambda i, j: (0, j))],
        out_specs=pl.BlockSpec((bm, bn), lambda i, j: (i, j)),
        out_shape=jax.ShapeDtypeStruct((T, Fh), BF16),
        compiler_params=_cparams(("parallel", "parallel")),
        name="ffn_up",
    )(x, wg, wu)


def _flash_loop(scores, v_ref, s_refs, *, M, tk, body_keys=FLASH_BODY_KEYS):
    S, Dv = v_ref.shape
    n = S // tk
    assert n % 2 == 0

    def absorb(c, s_ref, mx_cst, carry):
        m, l, acc = carry
        mx, cst = mx_cst
        off = pl.multiple_of(c * tk, tk)
        m_new = jnp.maximum(m, mx if cst is None else mx + cst)
        a = jnp.exp2(m - m_new)
        p = jnp.exp2(s_ref[...] - (m_new if cst is None else m_new - cst))
        l = a * l + jnp.sum(p, axis=0, keepdims=True)
        pv = lax.dot_general(v_ref[pl.ds(off, tk), :], p.astype(BF16), (((0,), (0,)), ((), ())),
                             preferred_element_type=F32)
        return m_new, l, a * acc + pv

    def body(jj, carry):
        mx0, state = carry
        c0 = 2 * jj
        mx1 = scores(c0 + 1, s_refs[1])
        state = absorb(c0, s_refs[0], mx0, state)
        mx0 = scores(jnp.minimum(c0 + 2, n - 1), s_refs[0])
        state = absorb(c0 + 1, s_refs[1], mx1, state)
        return mx0, state

    init = (jnp.full((1, M), -jnp.inf, F32), jnp.zeros((1, M), F32), jnp.zeros((Dv, M), F32))
    unroll = max(1, min(n // 2, body_keys // (2 * tk)))
    _, (_, l, acc) = lax.fori_loop(0, n // 2, body, (scores(0, s_refs[0]), init), unroll=unroll)
    return l, acc


def _flash_body(q, k_ref, v_ref, s_refs, *, tk):
    qt = q.astype(F32).T.astype(BF16)

    def scores(c, s_ref):
        off = pl.multiple_of(c * tk, tk)
        st = jnp.dot(k_ref[pl.ds(off, tk), :], qt, preferred_element_type=F32)
        s_ref[...] = st
        return jnp.max(st, axis=0, keepdims=True), None

    l, acc = _flash_loop(scores, v_ref, s_refs, M=q.shape[0], tk=tk)
    return (acc * (1.0 / l)).T


def _gqa_attn_kernel(q_ref, k_ref, v_ref, o_ref, s0_ref, s1_ref, *, G, tk):
    tq = q_ref.shape[0]
    q = jnp.concatenate([q_ref[:, g * LANE:(g + 1) * LANE] for g in range(G)], axis=0)
    o = _flash_body(q, k_ref, v_ref, (s0_ref, s1_ref), tk=tk)
    for g in range(G):
        o_ref[:, g * LANE:(g + 1) * LANE] = o[g * tq:(g + 1) * tq].astype(o_ref.dtype)


def _gqa_attn(q, k, v, *, B, S, tq=128, tk=1024):
    H, Hk = q.shape[1] // LANE, k.shape[1] // LANE
    G = H // Hk
    tk = min(tk, S // 2)
    assert S % tq == 0 and S % tk == 0
    q3, k3, v3 = (t.reshape(B, S, t.shape[1]) for t in (q, k, v))
    kern = functools.partial(_gqa_attn_kernel, G=G, tk=tk)
    out = pl.pallas_call(
        kern,
        grid=(B, Hk, S // tq),
        in_specs=[pl.BlockSpec((None, tq, G * LANE), lambda b, h, i: (b, i, h)),
                  pl.BlockSpec((None, S, LANE), lambda b, h, i: (b, 0, h)),
                  pl.BlockSpec((None, S, LANE), lambda b, h, i: (b, 0, h))],
        out_specs=pl.BlockSpec((None, tq, G * LANE), lambda b, h, i: (b, i, h)),
        out_shape=jax.ShapeDtypeStruct((B, S, H * LANE), BF16),
        scratch_shapes=[pltpu.VMEM((tk, G * tq), F32)] * 2,
        compiler_params=_cparams(("parallel", "parallel", "parallel")),
        name="gqa_attn",
    )(q3, k3, v3)
    return out.reshape(B * S, H * LANE)


def _mla_attn_kernel(q_ref, kn_ref, kr_ref, v_ref, o_ref, kc_ref, s0_ref, s1_ref, *, tk):
    @pl.when(pl.program_id(2) == 0)
    def _():
        kc_ref[:, :LANE] = kn_ref[...]
        kc_ref[:, LANE:] = kr_ref[...]

    o_ref[...] = _flash_body(q_ref[...], kc_ref, v_ref, (s0_ref, s1_ref), tk=tk).astype(o_ref.dtype)


def _mla_attn(q, kv, kr, *, B, S, tq=512, tk=1024):
    H = q.shape[1] // (2 * LANE)
    tq, tk = min(tq, S), min(tk, S // 2)
    assert S % tq == 0 and S % tk == 0
    q3, kv3, kr3 = q.reshape(B, S, -1), kv.reshape(B, S, -1), kr.reshape(B, S, LANE)
    kern = functools.partial(_mla_attn_kernel, tk=tk)
    out = pl.pallas_call(
        kern,
        grid=(B, H, S // tq),
        in_specs=[pl.BlockSpec((None, tq, 2 * LANE), lambda b, h, i: (b, i, h)),
                  pl.BlockSpec((None, S, LANE), lambda b, h, i: (b, 0, 2 * h)),
                  pl.BlockSpec((None, S, LANE), lambda b, h, i: (b, 0, 0)),
                  pl.BlockSpec((None, S, LANE), lambda b, h, i: (b, 0, 2 * h + 1))],
        out_specs=pl.BlockSpec((None, tq, LANE), lambda b, h, i: (b, i, h)),
        out_shape=jax.ShapeDtypeStruct((B, S, H * LANE), BF16),
        scratch_shapes=[pltpu.VMEM((S, 2 * LANE), BF16)] + [pltpu.VMEM((tk, tq), F32)] * 2,
        compiler_params=_cparams(("arbitrary", "arbitrary", "arbitrary")),
        name="mla_attn",
    )(q3, kv3, kr3, kv3)
    return out.reshape(B * S, H * LANE)


def _diff_attn_kernel(slopes_ref, q_ref, k_ref, v_ref, lq1_ref, lk1_ref, lq2_ref, lk2_ref, sub_ref, o_ref,
                      s0_ref, s1_ref, t_ref, *, tk, lam_init):
    tq = q_ref.shape[0]
    slope = slopes_ref[pl.program_id(1)]
    q0 = pl.program_id(2) * tq
    c_diag = q0 // tk
    arel = slope * (lax.broadcasted_iota(jnp.int32, (tk, tq), 0)
                    - lax.broadcasted_iota(jnp.int32, (tk, tq), 1)).astype(F32)
    t_ref[0] = arel
    t_ref[1] = -arel
    t_ref[2] = -jnp.abs(arel + slope * (c_diag * tk - q0).astype(F32))
    qts = [q_ref[:, c * DIFF_HD:(c + 1) * DIFF_HD].astype(F32).T.astype(BF16) for c in range(2)]

    def scores(c, s_ref):
        off = pl.multiple_of(c * tk, tk)
        e = slope * (off - q0).astype(F32)
        side = jnp.where(c == c_diag, 2, jnp.where(c > c_diag, 1, 0))
        cst = jnp.where(c == c_diag, 0.0, jnp.where(c > c_diag, -e, e))
        t = t_ref[side]
        mxs = []
        for comp in range(2):
            ks = k_ref[pl.ds(off, tk), comp * DIFF_HD:(comp + 1) * DIFF_HD]
            st = jnp.dot(ks, qts[comp], preferred_element_type=F32) + t
            s_ref[:, comp * tq:(comp + 1) * tq] = st
            mxs.append(jnp.max(st, axis=0, keepdims=True))
        return jnp.concatenate(mxs, axis=1), cst

    l, acc = _flash_loop(scores, v_ref, (s0_ref, s1_ref), M=2 * tq, tk=tk, body_keys=DIFF_BODY_KEYS)
    on = acc * (1.0 / l)
    lam = (jnp.exp(jnp.sum(lq1_ref[...] * lk1_ref[...], axis=-1, keepdims=True))
           - jnp.exp(jnp.sum(lq2_ref[...] * lk2_ref[...], axis=-1, keepdims=True)) + lam_init)
    o = (on[:, :tq] - lam * on[:, tq:]).T
    ms = jnp.mean(o * o, axis=-1, keepdims=True)
    o = o * lax.rsqrt(ms + RMS_EPS) * sub_ref[...] * (1.0 - lam_init)
    o_ref[...] = o.astype(o_ref.dtype)


def _diff_attn(q, kv, lam_vecs, subln, *, B, S, lam_init, tq=256, tk=512):
    W = 2 * DIFF_HD
    H = q.shape[1] // W
    tq, tk = min(tq, S), min(tk, S // 2)
    assert S % tq == 0 and S % tk == 0 and tk % tq == 0
    slopes = LOG2E * jnp.exp2(-8.0 * jnp.arange(1, H + 1, dtype=F32) / H)
    q3, kv3 = q.reshape(B, S, -1), kv.reshape(B, S, -1)
    vec = pl.BlockSpec((1, DIFF_HD), lambda b, h, i: (0, 0))
    kern = functools.partial(_diff_attn_kernel, tk=tk, lam_init=lam_init)
    out = pl.pallas_call(
        kern,
        grid=(B, H, S // tq),
        in_specs=[pl.BlockSpec(memory_space=pltpu.SMEM),
                  pl.BlockSpec((None, tq, W), lambda b, h, i: (b, i, h)),
                  pl.BlockSpec((None, S, W), lambda b, h, i: (b, 0, h)),
                  pl.BlockSpec((None, S, W), lambda b, h, i: (b, 0, H + h)),
                  vec, vec, vec, vec,
                  pl.BlockSpec((1, W), lambda b, h, i: (0, 0))],
        out_specs=pl.BlockSpec((None, tq, W), lambda b, h, i: (b, i, h)),
        out_shape=jax.ShapeDtypeStruct((B, S, H * W), BF16),
        scratch_shapes=[pltpu.VMEM((tk, 2 * tq), F32)] * 2 + [pltpu.VMEM((3, tk, tq), F32)],
        compiler_params=_cparams(("parallel", "parallel", "parallel")),
        name="diff_attn",
    )(slopes, q3, kv3, kv3, *[v.reshape(1, DIFF_HD).astype(F32) for v in lam_vecs],
      subln.reshape(1, W).astype(F32))
    return out.reshape(B * S, H * W)


def _na_rep_block(p, nblk):
    return jnp.where(p < 3, p, nblk - NA_WIN_BLOCKS + p)


def _na_bias_kernel(rpbT_ref, o_ref, *, nblk, R):
    nk = NA_WIN_BLOCKS * Q_BLOCK
    rows_per_blk = Q_BLOCK // GRID_W
    kh = min(NA_KH_MAX, R)
    j = _na_rep_block(pl.program_id(0), nblk)
    ws = jnp.clip(j - 2, 0, nblk - NA_WIN_BLOCKS)
    tab = rpbT_ref[...]
    n_dr = 2 * NA_KH_MAX - 1
    n_dc = 2 * NA_KW - 1

    w_shift = GRID_W.bit_length() - 1
    n1 = lax.broadcasted_iota(jnp.int32, (32, nk), 1)
    krow1 = rows_per_blk * ws + (n1 >> w_shift)
    for a in range(rows_per_blk):
        qrow = rows_per_blk * j + a
        drow1 = jnp.clip(krow1 - qrow + NA_KH_MAX - 1, 0, n_dr - 1)
        sel = jnp.zeros((32, nk), F32)
        for dr in range(n_dr):
            sel = jnp.where(drow1 == dr, tab[:, dr:dr + 1], sel)
        qc = lax.broadcasted_iota(jnp.int32, (GRID_W, nk), 0)
        n = lax.broadcasted_iota(jnp.int32, (GRID_W, nk), 1)
        kc = n & (GRID_W - 1)
        krow = rows_per_blk * ws + (n >> w_shift)
        rs = jnp.clip(qrow - kh // 2, 0, R - kh)
        cs = jnp.clip(qc - NA_KW // 2, 0, GRID_W - NA_KW)
        ok = (kc >= cs) & (kc < cs + NA_KW) & (krow >= rs) & (krow < rs + kh)
        dcol = jnp.clip(kc - qc + NA_KW - 1, 0, n_dc - 1)
        bias = jnp.zeros((GRID_W, nk), F32)
        for dc in range(n_dc):
            bias = jnp.where(dcol == dc, sel[dc:dc + 1, :], bias)
        o_ref[a * GRID_W:(a + 1) * GRID_W, :] = jnp.where(ok, bias, NEG_INF)


def _na_bias(rpb, *, S):
    H = rpb.shape[0]
    nblk, R = S // Q_BLOCK, S // GRID_W
    nk = NA_WIN_BLOCKS * Q_BLOCK
    rpbT = jnp.transpose(rpb.astype(F32), (0, 2, 1))
    rpbT = jnp.pad(rpbT, ((0, 0), (0, 32 - rpbT.shape[1]), (0, LANE - rpbT.shape[2])))
    kern = functools.partial(_na_bias_kernel, nblk=nblk, R=R)
    return pl.pallas_call(
        kern,
        grid=(NA_WIN_BLOCKS, H),
        in_specs=[pl.BlockSpec((None, 32, LANE), lambda p, h: (h, 0, 0))],
        out_specs=pl.BlockSpec((None, None, Q_BLOCK, nk), lambda p, h: (p, h, 0, 0)),
        out_shape=jax.ShapeDtypeStruct((NA_WIN_BLOCKS, H, Q_BLOCK, nk), F32),
        compiler_params=_cparams(("parallel", "parallel")),
        name="na_bias",
    )(rpbT)


def _na_attn_kernel(q_ref, *refs, H):
    k_refs = refs[:NA_WIN_BLOCKS]
    v_refs = refs[NA_WIN_BLOCKS:2 * NA_WIN_BLOCKS]
    bias_ref, o_ref = refs[2 * NA_WIN_BLOCKS], refs[2 * NA_WIN_BLOCKS + 1]
    for h in range(H):
        hs = slice(h * LANE, (h + 1) * LANE)
        k = jnp.concatenate([r[:, hs] for r in k_refs], axis=0)
        v = jnp.concatenate([r[:, hs] for r in v_refs], axis=0)
        s = lax.dot_general(q_ref[:, hs], k, (((1,), (1,)), ((), ())), preferred_element_type=F32)
        s = s + bias_ref[h]
        m = jnp.max(s, axis=-1, keepdims=True)
        p = jnp.exp(s - m)
        l = jnp.sum(p, axis=-1, keepdims=True)
        o = jnp.dot(p.astype(BF16), v, preferred_element_type=F32) / l
        o_ref[:, hs] = o.astype(o_ref.dtype)


def _na_attn(q, kv, bias, *, B, S):
    D = q.shape[1]
    H = D // LANE
    nblk = S // Q_BLOCK
    assert S % Q_BLOCK == 0 and nblk >= NA_WIN_BLOCKS
    q3, kv3 = q.reshape(B, S, D), kv.reshape(B, S, 2 * D)

    def win(j):
        return jnp.clip(j - 2, 0, nblk - NA_WIN_BLOCKS)

    def pattern(j):
        return jnp.where(j < 2, j, jnp.where(j <= nblk - 3, 2, j - (nblk - NA_WIN_BLOCKS)))

    k_specs = [pl.BlockSpec((None, Q_BLOCK, D), lambda b, j, t=t: (b, win(j) + t, 0))
               for t in range(NA_WIN_BLOCKS)]
    v_specs = [pl.BlockSpec((None, Q_BLOCK, D), lambda b, j, t=t: (b, win(j) + t, 1))
               for t in range(NA_WIN_BLOCKS)]
    nk = NA_WIN_BLOCKS * Q_BLOCK
    out = pl.pallas_call(
        functools.partial(_na_attn_kernel, H=H),
        grid=(B, nblk),
        in_specs=[pl.BlockSpec((None, Q_BLOCK, D), lambda b, j: (b, j, 0))] + k_specs + v_specs
                 + [pl.BlockSpec((None, H, Q_BLOCK, nk), lambda b, j: (pattern(j), 0, 0, 0))],
        out_specs=pl.BlockSpec((None, Q_BLOCK, D), lambda b, j: (b, j, 0)),
        out_shape=jax.ShapeDtypeStruct((B, S, D), BF16),
        compiler_params=_cparams(("parallel", "parallel")),
        name="na_attn",
    )(q3, *([kv3] * (2 * NA_WIN_BLOCKS)), bias)
    return out.reshape(B * S, D)


def _rope_cos_sin(pos, dim):
    inv = 1.0 / (ROPE_THETA ** (jnp.arange(0, dim, 2, dtype=F32) / dim))
    ang = pos.astype(F32)[:, None] * inv[None, :]
    return jnp.cos(ang), jnp.sin(ang)


def _rope_tables(pairs, S):
    z = jnp.zeros((S, 32), F32)
    c_parts, s1_parts, s2_parts = [], [], []
    for cos, sin in pairs:
        c_parts += [cos, cos]
        s1_parts += [-sin, z]
        s2_parts += [z, sin]
    pad = LANE - 64 * len(pairs)
    if pad:
        zp = jnp.zeros((S, pad), F32)
        c_parts.append(zp), s1_parts.append(zp), s2_parts.append(zp)
    return tuple(jnp.concatenate(p, axis=1) for p in (c_parts, s1_parts, s2_parts))


def _mla_mixer(h, hb, B, S, w_in, q_norm, w_q_up, kv_norm, w_kv_up):
    QR, KVR = q_norm.shape[0], kv_norm.shape[0]
    H = w_q_up.shape[1] // (MLA_NOPE + MLA_ROPE)
    tabs = _rope_tables([_rope_cos_sin(jnp.arange(S), MLA_ROPE)], S)
    w_in = w_in.astype(BF16)
    w_kr = jnp.pad(w_in[:, QR + KVR:], ((0, 0), (0, LANE - MLA_ROPE)))
    cq = _proj(hb, w_in[:, :QR], seq=S, norm="full", gain=q_norm)
    ckv = _proj(hb, w_in[:, QR:QR + KVR], seq=S, norm="full", gain=kv_norm)
    kr = _proj(hb, w_kr, seq=S, rope="all", tabs=tabs)
    wq = w_q_up.astype(BF16).reshape(QR, H, MLA_NOPE + MLA_ROPE)
    wq = jnp.pad(wq, ((0, 0), (0, 0), (0, 2 * LANE - MLA_NOPE - MLA_ROPE))).reshape(QR, H * 2 * LANE)
    q = _proj(cq, wq, seq=S, rope="odd", tabs=tabs, scale=LOG2E * (MLA_NOPE + MLA_ROPE) ** -0.5)
    kv = _proj(ckv, w_kv_up.astype(BF16), seq=S)
    return _mla_attn(q, kv, kr, B=B, S=S)


def _gqa_mixer(h, hb, B, S, w_qkv, q_norm, k_norm):
    HD = q_norm.shape[0]
    Hk = GQA_KV_HEADS
    nq = w_qkv.shape[1] - 2 * Hk * HD
    half = HD // 2
    pos = jnp.arange(S)
    tabs = _rope_tables([_rope_cos_sin(pos // GRID_W, half), _rope_cos_sin(pos % GRID_W, half)], S)
    w = w_qkv.astype(BF16)
    q = _proj(hb, w[:, :nq], seq=S, norm="group", gain=q_norm, rope="all", tabs=tabs, scale=LOG2E * HD ** -0.5)
    k = _proj(hb, w[:, nq:nq + Hk * HD], seq=S, norm="group", gain=k_norm, rope="all", tabs=tabs)
    v = _proj(hb, w[:, nq + Hk * HD:], seq=S)
    return _gqa_attn(q, k, v, B=B, S=S)


def _na_mixer(h, hb, B, S, w_qkv, rpb):
    D = w_qkv.shape[1] // 3
    w = w_qkv.astype(BF16)
    q = _proj(hb, w[:, :D], seq=S, scale=LANE ** -0.5)
    kv = _proj(hb, w[:, D:], seq=S)
    return _na_attn(q, kv, _na_bias(rpb, S=S), B=B, S=S)


def _diff_mixer(h, hb, B, S, w_qkv, lam_vecs, subln, lam_init):
    D = w_qkv.shape[1] // 3
    w = w_qkv.astype(BF16)
    q = _proj(hb, w[:, :D], seq=S, scale=LOG2E * DIFF_HD ** -0.5)
    kv = _proj(hb, w[:, D:], seq=S)
    return _diff_attn(q, kv, lam_vecs, subln, B=B, S=S, lam_init=lam_init)


def kernel(x, mla_w_in, mla_q_norm, mla_w_q_up, mla_kv_norm, mla_w_kv_up, mla_w_o, gqa_w_qkv, gqa_q_norm, gqa_k_norm, gqa_w_o, na_w_qkv, na_rpb, na_w_o, diff_w_qkv, diff_lambda_q1, diff_lambda_k1, diff_lambda_q2, diff_lambda_k2, diff_subln, diff_w_o, ffn_w_gate, ffn_w_up, ffn_w_down, ln_mix_g, ln_mix_b, ln_ffn_g, ln_ffn_b):
    B, S, D = x.shape
    depth = ffn_w_gate.shape[0]
    alpha = (2.0 * depth) ** 0.25
    h = x.reshape(B * S, D)
    hb = h
    for i in range(depth):
        m = i % 4
        if m == 0:
            o = _mla_mixer(h, hb, B, S, mla_w_in, mla_q_norm, mla_w_q_up, mla_kv_norm, mla_w_kv_up)
            w_o = mla_w_o
        elif m == 1:
            o = _gqa_mixer(h, hb, B, S, gqa_w_qkv, gqa_q_norm, gqa_k_norm)
            w_o = gqa_w_o
        elif m == 2:
            o = _na_mixer(h, hb, B, S, na_w_qkv, na_rpb)
            w_o = na_w_o
        else:
            lam_init = 0.8 - 0.6 * math.exp(-0.3 * i)
            o = _diff_mixer(h, hb, B, S, diff_w_qkv,
                            (diff_lambda_q1, diff_lambda_k1, diff_lambda_q2, diff_lambda_k2),
                            diff_subln, lam_init)
            w_o = diff_w_o
        h, hb = _mm_ln(o, w_o.astype(BF16), h, ln_mix_g[i], ln_mix_b[i], alpha=alpha)
        hid = _ffn_up(hb, ffn_w_gate[i].astype(BF16), ffn_w_up[i].astype(BF16))
        h, hb = _mm_ln(hid, ffn_w_down[i].astype(BF16), h, ln_ffn_g[i], ln_ffn_b[i], alpha=alpha)
    return h.reshape(B, S, D)
```

```python
import functools
import math

import jax
import jax.numpy as jnp
from jax import lax
from jax.experimental import pallas as pl
from jax.experimental.pallas import tpu as pltpu

F32 = jnp.float32
BF16 = jnp.bfloat16

LANE = 128
VMEM_LIMIT_BYTES = 56 * 1024 * 1024

GRID_W = 64
Q_BLOCK = 128
ROPE_THETA = 10000.0
NEG_INF = -1e30
MLA_NOPE, MLA_ROPE, MLA_V = 128, 64, 128
GQA_KV_HEADS = 4
NA_KH_MAX, NA_KW = 8, 16
NA_WIN_BLOCKS = 5
DIFF_HD = 128
LN_EPS = 1e-5
RMS_EPS = 1e-6
LOG2E = math.log2(math.e)
FLASH_BODY_KEYS = 16384
PROJ_ROW_BLOCK = 256
DIFF_BODY_KEYS = 8192


def _cparams(sem, **kw):
    return pltpu.CompilerParams(dimension_semantics=sem, vmem_limit_bytes=VMEM_LIMIT_BYTES, **kw)


def _proj_kernel(*refs, n_groups, norm, rope, scale):
    it = iter(refs)
    x_ref, w_ref = next(it), next(it)
    g_ref = next(it) if norm is not None else None
    tabs = (next(it), next(it), next(it)) if rope is not None else None
    o_ref = next(it)
    bm = x_ref.shape[0]
    rb = min(bm, PROJ_ROW_BLOCK)
    for r in range(bm // rb):
        rows = slice(r * rb, (r + 1) * rb)
        acc = jnp.dot(x_ref[rows, :].astype(BF16), w_ref[...], preferred_element_type=F32)
        if norm == "full":
            ms = jnp.mean(acc * acc, axis=-1, keepdims=True)
            acc = acc * lax.rsqrt(ms + RMS_EPS) * g_ref[...]
        for g in range(n_groups):
            a = acc[:, g * LANE:(g + 1) * LANE]
            if norm == "group":
                ms = jnp.mean(a * a, axis=-1, keepdims=True)
                a = a * lax.rsqrt(ms + RMS_EPS) * g_ref[...]
            if rope == "all" or (rope == "odd" and g % 2 == 1):
                c, s1, s2 = tabs[0][rows, :], tabs[1][rows, :], tabs[2][rows, :]
                a = a * c + pltpu.roll(a, 96, 1) * s1 + pltpu.roll(a, 32, 1) * s2
            if scale != 1.0:
                a = a * scale
            o_ref[rows, g * LANE:(g + 1) * LANE] = a.astype(o_ref.dtype)


def _proj(x, w, *, seq, norm=None, gain=None, rope=None, tabs=None, scale=1.0, bm=1024, bn=None):
    T, K = x.shape
    N = w.shape[1]
    if bn is None:
        bn = N if N <= 1024 else 1024
    bm = min(bm, seq)
    assert T % bm == 0 and seq % bm == 0 and N % bn == 0 and bn % LANE == 0
    if norm == "full":
        assert bn == N
    if rope == "odd":
        assert (bn // LANE) % 2 == 0
    in_specs = [pl.BlockSpec((bm, K), lambda i, j: (i, 0)),
                pl.BlockSpec((K, bn), lambda i, j: (0, j))]
    args = [x, w]
    if norm == "full":
        in_specs.append(pl.BlockSpec((1, bn), lambda i, j: (0, 0)))
        args.append(gain.reshape(1, N).astype(F32))
    elif norm == "group":
        in_specs.append(pl.BlockSpec((1, LANE), lambda i, j: (0, 0)))
        args.append(gain.reshape(1, LANE).astype(F32))
    if rope is not None:
        nseq = seq // bm
        for t in tabs:
            in_specs.append(pl.BlockSpec((bm, LANE), lambda i, j: (i % nseq, 0)))
            args.append(t)
    kern = functools.partial(_proj_kernel, n_groups=bn // LANE, norm=norm, rope=rope, scale=scale)
    return pl.pallas_call(
        kern,
        grid=(T // bm, N // bn),
        in_specs=in_specs,
        out_specs=pl.BlockSpec((bm, bn), lambda i, j: (i, j)),
        out_shape=jax.ShapeDtypeStruct((T, N), BF16),
        compiler_params=_cparams(("parallel", "parallel")),
        name="proj",
    )(*args)


def _mm_ln_kernel(a_ref, w_ref, h_ref, g_ref, b_ref, of_ref, ob_ref, *, alpha):
    y = alpha * h_ref[...] + jnp.dot(a_ref[...], w_ref[...], preferred_element_type=F32)
    mu = jnp.mean(y, axis=-1, keepdims=True)
    d = y - mu
    var = jnp.mean(d * d, axis=-1, keepdims=True)
    out = d * lax.rsqrt(var + LN_EPS) * g_ref[...] + b_ref[...]
    of_ref[...] = out
    ob_ref[...] = out.astype(BF16)


def _mm_ln(a, w, h, g, b, *, alpha):
    T, K = a.shape
    D = w.shape[1]
    row_bytes = 2 * (2 * K) + 2 * (4 * D) + 2 * (4 * D) + 2 * (2 * D) + 4 * D
    budget = VMEM_LIMIT_BYTES - 2 * K * D - (8 << 20)
    bm = 512
    while bm > 8 and (bm * row_bytes > budget or T % bm):
        bm //= 2
    assert T % bm == 0
    resident = pl.Buffered(1)
    return pl.pallas_call(
        functools.partial(_mm_ln_kernel, alpha=alpha),
        grid=(T // bm,),
        in_specs=[pl.BlockSpec((bm, K), lambda i: (i, 0)),
                  pl.BlockSpec((K, D), lambda i: (0, 0), pipeline_mode=resident),
                  pl.BlockSpec((bm, D), lambda i: (i, 0)),
                  pl.BlockSpec((1, D), lambda i: (0, 0)),
                  pl.BlockSpec((1, D), lambda i: (0, 0))],
        out_specs=[pl.BlockSpec((bm, D), lambda i: (i, 0)),
                   pl.BlockSpec((bm, D), lambda i: (i, 0))],
        out_shape=[jax.ShapeDtypeStruct((T, D), F32), jax.ShapeDtypeStruct((T, D), BF16)],
        compiler_params=_cparams(("parallel",)),
        name="mm_ln",
    )(a, w, h, g.reshape(1, D).astype(F32), b.reshape(1, D).astype(F32))


def _ffn_up_kernel(x_ref, wg_ref, wu_ref, o_ref):
    x = x_ref[...]
    g = jnp.dot(x, wg_ref[...], preferred_element_type=F32)
    u = jnp.dot(x, wu_ref[...], preferred_element_type=F32)
    o_ref[...] = ((g / (1.0 + jnp.exp(-g))) * u).astype(o_ref.dtype)


def _ffn_up(x, wg, wu, *, bm=1024, bn=512):
    T, D = x.shape
    Fh = wg.shape[1]
    assert T % bm == 0 and Fh % bn == 0
    return pl.pallas_call(
        _ffn_up_kernel,
        grid=(T // bm, Fh // bn),
        in_specs=[pl.BlockSpec((bm, D), lambda i, j: (i, 0)),
                  pl.BlockSpec((D, bn), lambda i, j: (0, j)),
                  pl.BlockSpec((D, bn), lambda i, j: (0, j))],
        out_specs=pl.BlockSpec((bm, bn), lambda i, j: (i, j)),
        out_shape=jax.ShapeDtypeStruct((T, Fh), BF16),
        compiler_params=_cparams(("parallel", "parallel")),
        name="ffn_up",
    )(x, wg, wu)


def _chunk_start(c, tk):
    return c * tk if isinstance(c, int) else pl.multiple_of(c * tk, tk)


def _largest_divisor_leq(n, cap):
    return max(d for d in range(1, max(1, min(n, cap)) + 1) if n % d == 0)


def _flash_loop(score_fns, v_ref, s_refs, *, M, tk, body_keys=FLASH_BODY_KEYS):
    S, Dv = v_ref.shape
    n = S // tk
    assert n % 2 == 0
    npairs = n // 2

    def absorb(c, s_ref, mx_cst, carry):
        m, l, acc = carry
        mx, cst = mx_cst
        off = _chunk_start(c, tk)
        m_new = jnp.maximum(m, mx if cst is None else mx + cst)
        a = jnp.exp2(m - m_new)
        p = jnp.exp2(s_ref[...] - (m_new if cst is None else m_new - cst))
        l = a * l + jnp.sum(p, axis=0, keepdims=True)
        pv = lax.dot_general(v_ref[pl.ds(off, tk), :], p.astype(BF16), (((0,), (0,)), ((), ())),
                             preferred_element_type=F32)
        return m_new, l, a * acc + pv

    def pair(scores, issue_next, c0, mx0, state):
        mx1 = scores(c0 + 1, s_refs[1])
        state = absorb(c0, s_refs[0], mx0, state)
        mx0 = issue_next()
        state = absorb(c0 + 1, s_refs[1], mx1, state)
        return mx0, state

    init = (jnp.full((1, M), -jnp.inf, F32), jnp.zeros((1, M), F32), jnp.zeros((Dv, M), F32))
    out = []
    mx0 = score_fns[0](0, s_refs[0])
    for t, scores in enumerate(score_fns):
        state = init
        if npairs > 1:
            def body(jj, carry, scores=scores):
                return pair(scores, lambda: scores(2 * jj + 2, s_refs[0]), 2 * jj, *carry)

            unroll = _largest_divisor_leq(npairs - 1, max(1, body_keys // (2 * tk)))
            mx0, state = lax.fori_loop(0, npairs - 1, body, (mx0, state), unroll=unroll)
        if t + 1 < len(score_fns):
            issue_next = functools.partial(score_fns[t + 1], 0, s_refs[0])
        else:
            issue_next = lambda: None
        mx0, state = pair(scores, issue_next, 2 * (npairs - 1), mx0, state)
        out.append(state[1:])
    return out


def _flash_body(q_tiles, k_ref, v_ref, s_refs, *, tk):
    def make_scores(q):
        qt = q.astype(F32).T.astype(BF16)

        def scores(c, s_ref):
            st = jnp.dot(k_ref[pl.ds(_chunk_start(c, tk), tk), :], qt, preferred_element_type=F32)
            s_ref[...] = st
            return jnp.max(st, axis=0, keepdims=True), None

        return scores

    res = _flash_loop([make_scores(q) for q in q_tiles], v_ref, s_refs, M=q_tiles[0].shape[0], tk=tk)
    return [(acc * (1.0 / l)).T for l, acc in res]


def _gqa_attn_kernel(q_ref, k_ref, v_ref, o_ref, s0_ref, s1_ref, *, G, tq, tk):
    tiles = [slice(r, r + tq) for r in range(0, q_ref.shape[0], tq)]
    qs = [jnp.concatenate([q_ref[rows, g * LANE:(g + 1) * LANE] for g in range(G)], axis=0) for rows in tiles]
    for rows, o in zip(tiles, _flash_body(qs, k_ref, v_ref, (s0_ref, s1_ref), tk=tk)):
        for g in range(G):
            o_ref[rows, g * LANE:(g + 1) * LANE] = o[g * tq:(g + 1) * tq].astype(o_ref.dtype)


def _gqa_attn(q, k, v, *, B, S, tq=128, tk=1024, nt=2):
    H, Hk = q.shape[1] // LANE, k.shape[1] // LANE
    G = H // Hk
    tk = min(tk, S // 2)
    bq = nt * tq
    assert S % bq == 0 and S % tk == 0
    q3, k3, v3 = (t.reshape(B, S, t.shape[1]) for t in (q, k, v))
    kern = functools.partial(_gqa_attn_kernel, G=G, tq=tq, tk=tk)
    out = pl.pallas_call(
        kern,
        grid=(B, Hk, S // bq),
        in_specs=[pl.BlockSpec((None, bq, G * LANE), lambda b, h, i: (b, i, h)),
                  pl.BlockSpec((None, S, LANE), lambda b, h, i: (b, 0, h)),
                  pl.BlockSpec((None, S, LANE), lambda b, h, i: (b, 0, h))],
        out_specs=pl.BlockSpec((None, bq, G * LANE), lambda b, h, i: (b, i, h)),
        out_shape=jax.ShapeDtypeStruct((B, S, H * LANE), BF16),
        scratch_shapes=[pltpu.VMEM((tk, G * tq), F32)] * 2,
        compiler_params=_cparams(("parallel", "parallel", "parallel")),
        name="gqa_attn",
    )(q3, k3, v3)
    return out.reshape(B * S, H * LANE)


def _mla_attn_kernel(q_ref, kn_ref, kr_ref, v_ref, o_ref, kc_ref, s0_ref, s1_ref, *, tq, tk):
    @pl.when(pl.program_id(2) == 0)
    def _():
        kc_ref[:, :LANE] = kn_ref[...]
        kc_ref[:, LANE:] = kr_ref[...]

    tiles = [slice(r, r + tq) for r in range(0, q_ref.shape[0], tq)]
    outs = _flash_body([q_ref[rows, :] for rows in tiles], kc_ref, v_ref, (s0_ref, s1_ref), tk=tk)
    for rows, o in zip(tiles, outs):
        o_ref[rows, :] = o.astype(o_ref.dtype)


def _mla_attn(q, kv, kr, *, B, S, tq=512, tk=1024, nt=2):
    H = q.shape[1] // (2 * LANE)
    tq, tk = min(tq, S // nt), min(tk, S // 2)
    bq = nt * tq
    assert S % bq == 0 and S % tk == 0
    q3, kv3, kr3 = q.reshape(B, S, -1), kv.reshape(B, S, -1), kr.reshape(B, S, LANE)
    kern = functools.partial(_mla_attn_kernel, tq=tq, tk=tk)
    out = pl.pallas_call(
        kern,
        grid=(B, H, S // bq),
        in_specs=[pl.BlockSpec((None, bq, 2 * LANE), lambda b, h, i: (b, i, h)),
                  pl.BlockSpec((None, S, LANE), lambda b, h, i: (b, 0, 2 * h)),
                  pl.BlockSpec((None, S, LANE), lambda b, h, i: (b, 0, 0)),
                  pl.BlockSpec((None, S, LANE), lambda b, h, i: (b, 0, 2 * h + 1))],
        out_specs=pl.BlockSpec((None, bq, LANE), lambda b, h, i: (b, i, h)),
        out_shape=jax.ShapeDtypeStruct((B, S, H * LANE), BF16),
        scratch_shapes=[pltpu.VMEM((S, 2 * LANE), BF16)] + [pltpu.VMEM((tk, tq), F32)] * 2,
        compiler_params=_cparams(("arbitrary", "arbitrary", "arbitrary")),
        name="mla_attn",
    )(q3, kv3, kr3, kv3)
    return out.reshape(B * S, H * LANE)


def _diff_attn_kernel(slopes_ref, q_ref, k_ref, v_ref, lq1_ref, lk1_ref, lq2_ref, lk2_ref, sub_ref, o_ref,
                      s0_ref, s1_ref, t_ref, *, tq, tk, lam_init):
    slope = slopes_ref[pl.program_id(1)]
    arel = slope * (lax.broadcasted_iota(jnp.int32, (tk, tq), 0)
                    - lax.broadcasted_iota(jnp.int32, (tk, tq), 1)).astype(F32)
    t_ref[0] = arel
    t_ref[1] = -arel
    tiles = [slice(r, r + tq) for r in range(0, q_ref.shape[0], tq)]

    def make_scores(t, rows):
        q0 = pl.program_id(2) * q_ref.shape[0] + t * tq
        c_diag = q0 // tk
        t_ref[2 + t] = -jnp.abs(arel + slope * (c_diag * tk - q0).astype(F32))
        qts = [q_ref[rows, c * DIFF_HD:(c + 1) * DIFF_HD].astype(F32).T.astype(BF16) for c in range(2)]

        def scores(c, s_ref):
            off = _chunk_start(c, tk)
            e = slope * (off - q0).astype(F32)
            side = jnp.where(c == c_diag, 2 + t, jnp.where(c > c_diag, 1, 0))
            cst = jnp.where(c == c_diag, 0.0, jnp.where(c > c_diag, -e, e))
            bias = t_ref[side]
            mxs = []
            for comp in range(2):
                ks = k_ref[pl.ds(off, tk), comp * DIFF_HD:(comp + 1) * DIFF_HD]
                st = jnp.dot(ks, qts[comp], preferred_element_type=F32) + bias
                s_ref[:, comp * tq:(comp + 1) * tq] = st
                mxs.append(jnp.max(st, axis=0, keepdims=True))
            return jnp.concatenate(mxs, axis=1), cst

        return scores

    res = _flash_loop([make_scores(t, rows) for t, rows in enumerate(tiles)], v_ref, (s0_ref, s1_ref),
                      M=2 * tq, tk=tk, body_keys=DIFF_BODY_KEYS)
    lam = (jnp.exp(jnp.sum(lq1_ref[...] * lk1_ref[...], axis=-1, keepdims=True))
           - jnp.exp(jnp.sum(lq2_ref[...] * lk2_ref[...], axis=-1, keepdims=True)) + lam_init)
    for rows, (l, acc) in zip(tiles, res):
        on = acc * (1.0 / l)
        o = (on[:, :tq] - lam * on[:, tq:]).T
        ms = jnp.mean(o * o, axis=-1, keepdims=True)
        o = o * lax.rsqrt(ms + RMS_EPS) * sub_ref[...] * (1.0 - lam_init)
        o_ref[rows, :] = o.astype(o_ref.dtype)


def _diff_attn(q, kv, lam_vecs, subln, *, B, S, lam_init, tq=256, tk=512, nt=4):
    W = 2 * DIFF_HD
    H = q.shape[1] // W
    tq, tk = min(tq, S // nt), min(tk, S // 2)
    bq = nt * tq
    assert S % bq == 0 and S % tk == 0 and tk % tq == 0
    slopes = LOG2E * jnp.exp2(-8.0 * jnp.arange(1, H + 1, dtype=F32) / H)
    q3, kv3 = q.reshape(B, S, -1), kv.reshape(B, S, -1)
    vec = pl.BlockSpec((1, DIFF_HD), lambda b, h, i: (0, 0))
    kern = functools.partial(_diff_attn_kernel, tq=tq, tk=tk, lam_init=lam_init)
    out = pl.pallas_call(
        kern,
        grid=(B, H, S // bq),
        in_specs=[pl.BlockSpec(memory_space=pltpu.SMEM),
                  pl.BlockSpec((None, bq, W), lambda b, h, i: (b, i, h)),
                  pl.BlockSpec((None, S, W), lambda b, h, i: (b, 0, h)),
                  pl.BlockSpec((None, S, W), lambda b, h, i: (b, 0, H + h)),
                  vec, vec, vec, vec,
                  pl.BlockSpec((1, W), lambda b, h, i: (0, 0))],
        out_specs=pl.BlockSpec((None, bq, W), lambda b, h, i: (b, i, h)),
        out_shape=jax.ShapeDtypeStruct((B, S, H * W), BF16),
        scratch_shapes=[pltpu.VMEM((tk, 2 * tq), F32)] * 2
                       + [pltpu.VMEM((2 + nt, tk, tq), F32)],
        compiler_params=_cparams(("parallel", "parallel", "parallel")),
        name="diff_attn",
    )(slopes, q3, kv3, kv3, *[v.reshape(1, DIFF_HD).astype(F32) for v in lam_vecs],
      subln.reshape(1, W).astype(F32))
    return out.reshape(B * S, H * W)


def _na_rep_block(p, nblk):
    return jnp.where(p < 3, p, nblk - NA_WIN_BLOCKS + p)


def _na_bias_kernel(rpbT_ref, o_ref, *, nblk, R):
    nk = NA_WIN_BLOCKS * Q_BLOCK
    rows_per_blk = Q_BLOCK // GRID_W
    kh = min(NA_KH_MAX, R)
    j = _na_rep_block(pl.program_id(0), nblk)
    ws = jnp.clip(j - 2, 0, nblk - NA_WIN_BLOCKS)
    tab = rpbT_ref[...]
    n_dr = 2 * NA_KH_MAX - 1
    n_dc = 2 * NA_KW - 1

    w_shift = GRID_W.bit_length() - 1
    n1 = lax.broadcasted_iota(jnp.int32, (32, nk), 1)
    krow1 = rows_per_blk * ws + (n1 >> w_shift)
    for a in range(rows_per_blk):
        qrow = rows_per_blk * j + a
        drow1 = jnp.clip(krow1 - qrow + NA_KH_MAX - 1, 0, n_dr - 1)
        sel = jnp.zeros((32, nk), F32)
        for dr in range(n_dr):
            sel = jnp.where(drow1 == dr, tab[:, dr:dr + 1], sel)
        qc = lax.broadcasted_iota(jnp.int32, (GRID_W, nk), 0)
        n = lax.broadcasted_iota(jnp.int32, (GRID_W, nk), 1)
        kc = n & (GRID_W - 1)
        krow = rows_per_blk * ws + (n >> w_shift)
        rs = jnp.clip(qrow - kh // 2, 0, R - kh)
        cs = jnp.clip(qc - NA_KW // 2, 0, GRID_W - NA_KW)
        ok = (kc >= cs) & (kc < cs + NA_KW) & (krow >= rs) & (krow < rs + kh)
        dcol = jnp.clip(kc - qc + NA_KW - 1, 0, n_dc - 1)
        bias = jnp.zeros((GRID_W, nk), F32)
        for dc in range(n_dc):
            bias = jnp.where(dcol == dc, sel[dc:dc + 1, :], bias)
        o_ref[a * GRID_W:(a + 1) * GRID_W, :] = jnp.where(ok, bias, NEG_INF)


def _na_bias(rpb, *, S):
    H = rpb.shape[0]
    nblk, R = S // Q_BLOCK, S // GRID_W
    nk = NA_WIN_BLOCKS * Q_BLOCK
    rpbT = jnp.transpose(rpb.astype(F32), (0, 2, 1))
    rpbT = jnp.pad(rpbT, ((0, 0), (0, 32 - rpbT.shape[1]), (0, LANE - rpbT.shape[2])))
    kern = functools.partial(_na_bias_kernel, nblk=nblk, R=R)
    return pl.pallas_call(
        kern,
        grid=(NA_WIN_BLOCKS, H),
        in_specs=[pl.BlockSpec((None, 32, LANE), lambda p, h: (h, 0, 0))],
        out_specs=pl.BlockSpec((None, None, Q_BLOCK, nk), lambda p, h: (p, h, 0, 0)),
        out_shape=jax.ShapeDtypeStruct((NA_WIN_BLOCKS, H, Q_BLOCK, nk), F32),
        compiler_params=_cparams(("parallel", "parallel")),
        name="na_bias",
    )(rpbT)


def _na_attn_kernel(q_ref, *refs, H):
    k_refs = refs[:NA_WIN_BLOCKS]
    v_refs = refs[NA_WIN_BLOCKS:2 * NA_WIN_BLOCKS]
    bias_ref, o_ref = refs[2 * NA_WIN_BLOCKS], refs[2 * NA_WIN_BLOCKS + 1]
    for h in range(H):
        hs = slice(h * LANE, (h + 1) * LANE)
        k = jnp.concatenate([r[:, hs] for r in k_refs], axis=0)
        v = jnp.concatenate([r[:, hs] for r in v_refs], axis=0)
        s = lax.dot_general(q_ref[:, hs], k, (((1,), (1,)), ((), ())), preferred_element_type=F32)
        s = s + bias_ref[h]
        m = jnp.max(s, axis=-1, keepdims=True)
        p = jnp.exp(s - m)
        l = jnp.sum(p, axis=-1, keepdims=True)
        o = jnp.dot(p.astype(BF16), v, preferred_element_type=F32) / l
        o_ref[:, hs] = o.astype(o_ref.dtype)


def _na_attn(q, kv, bias, *, B, S):
    D = q.shape[1]
    H = D // LANE
    nblk = S // Q_BLOCK
    assert S % Q_BLOCK == 0 and nblk >= NA_WIN_BLOCKS
    q3, kv3 = q.reshape(B, S, D), kv.reshape(B, S, 2 * D)

    def win(j):
        return jnp.clip(j - 2, 0, nblk - NA_WIN_BLOCKS)

    def pattern(j):
        return jnp.where(j < 2, j, jnp.where(j <= nblk - 3, 2, j - (nblk - NA_WIN_BLOCKS)))

    k_specs = [pl.BlockSpec((None, Q_BLOCK, D), lambda b, j, t=t: (b, win(j) + t, 0))
               for t in range(NA_WIN_BLOCKS)]
    v_specs = [pl.BlockSpec((None, Q_BLOCK, D), lambda b, j, t=t: (b, win(j) + t, 1))
               for t in range(NA_WIN_BLOCKS)]
    nk = NA_WIN_BLOCKS * Q_BLOCK
    out = pl.pallas_call(
        functools.partial(_na_attn_kernel, H=H),
        grid=(B, nblk),
        in_specs=[pl.BlockSpec((None, Q_BLOCK, D), lambda b, j: (b, j, 0))] + k_specs + v_specs
                 + [pl.BlockSpec((None, H, Q_BLOCK, nk), lambda b, j: (pattern(j), 0, 0, 0))],
        out_specs=pl.BlockSpec((None, Q_BLOCK, D), lambda b, j: (b, j, 0)),
        out_shape=jax.ShapeDtypeStruct((B, S, D), BF16),
        compiler_params=_cparams(("parallel", "parallel")),
        name="na_attn",
    )(q3, *([kv3] * (2 * NA_WIN_BLOCKS)), bias)
    return out.reshape(B * S, D)


def _rope_cos_sin(pos, dim):
    inv = 1.0 / (ROPE_THETA ** (jnp.arange(0, dim, 2, dtype=F32) / dim))
    ang = pos.astype(F32)[:, None] * inv[None, :]
    return jnp.cos(ang), jnp.sin(ang)


def _rope_tables(pairs, S):
    z = jnp.zeros((S, 32), F32)
    c_parts, s1_parts, s2_parts = [], [], []
    for cos, sin in pairs:
        c_parts += [cos, cos]
        s1_parts += [-sin, z]
        s2_parts += [z, sin]
    pad = LANE - 64 * len(pairs)
    if pad:
        zp = jnp.zeros((S, pad), F32)
        c_parts.append(zp), s1_parts.append(zp), s2_parts.append(zp)
    return tuple(jnp.concatenate(p, axis=1) for p in (c_parts, s1_parts, s2_parts))


def _mla_mixer(h, hb, B, S, w_in, q_norm, w_q_up, kv_norm, w_kv_up):
    QR, KVR = q_norm.shape[0], kv_norm.shape[0]
    H = w_q_up.shape[1] // (MLA_NOPE + MLA_ROPE)
    tabs = _rope_tables([_rope_cos_sin(jnp.arange(S), MLA_ROPE)], S)
    w_in = w_in.astype(BF16)
    w_kr = jnp.pad(w_in[:, QR + KVR:], ((0, 0), (0, LANE - MLA_ROPE)))
    cq = _proj(hb, w_in[:, :QR], seq=S, norm="full", gain=q_norm)
    ckv = _proj(hb, w_in[:, QR:QR + KVR], seq=S, norm="full", gain=kv_norm)
    kr = _proj(hb, w_kr, seq=S, rope="all", tabs=tabs)
    wq = w_q_up.astype(BF16).reshape(QR, H, MLA_NOPE + MLA_ROPE)
    wq = jnp.pad(wq, ((0, 0), (0, 0), (0, 2 * LANE - MLA_NOPE - MLA_ROPE))).reshape(QR, H * 2 * LANE)
    q = _proj(cq, wq, seq=S, rope="odd", tabs=tabs, scale=LOG2E * (MLA_NOPE + MLA_ROPE) ** -0.5)
    kv = _proj(ckv, w_kv_up.astype(BF16), seq=S)
    return _mla_attn(q, kv, kr, B=B, S=S)


def _gqa_mixer(h, hb, B, S, w_qkv, q_norm, k_norm):
    HD = q_norm.shape[0]
    Hk = GQA_KV_HEADS
    nq = w_qkv.shape[1] - 2 * Hk * HD
    half = HD // 2
    pos = jnp.arange(S)
    tabs = _rope_tables([_rope_cos_sin(pos // GRID_W, half), _rope_cos_sin(pos % GRID_W, half)], S)
    w = w_qkv.astype(BF16)
    q = _proj(hb, w[:, :nq], seq=S, norm="group", gain=q_norm, rope="all", tabs=tabs, scale=LOG2E * HD ** -0.5)
    k = _proj(hb, w[:, nq:nq + Hk * HD], seq=S, norm="group", gain=k_norm, rope="all", tabs=tabs)
    v = _proj(hb, w[:, nq + Hk * HD:], seq=S)
    return _gqa_attn(q, k, v, B=B, S=S)


def _na_mixer(h, hb, B, S, w_qkv, rpb):
    D = w_qkv.shape[1] // 3
    w = w_qkv.astype(BF16)
    q = _proj(hb, w[:, :D], seq=S, scale=LANE ** -0.5)
    kv = _proj(hb, w[:, D:], seq=S)
    return _na_attn(q, kv, _na_bias(rpb, S=S), B=B, S=S)


def _diff_mixer(h, hb, B, S, w_qkv, lam_vecs, subln, lam_init):
    D = w_qkv.shape[1] // 3
    w = w_qkv.astype(BF16)
    q = _proj(hb, w[:, :D], seq=S, scale=LOG2E * DIFF_HD ** -0.5)
    kv = _proj(hb, w[:, D:], seq=S)
    return _diff_attn(q, kv, lam_vecs, subln, B=B, S=S, lam_init=lam_init)


def kernel(x, mla_w_in, mla_q_norm, mla_w_q_up, mla_kv_norm, mla_w_kv_up, mla_w_o, gqa_w_qkv, gqa_q_norm, gqa_k_norm, gqa_w_o, na_w_qkv, na_rpb, na_w_o, diff_w_qkv, diff_lambda_q1, diff_lambda_k1, diff_lambda_q2, diff_lambda_k2, diff_subln, diff_w_o, ffn_w_gate, ffn_w_up, ffn_w_down, ln_mix_g, ln_mix_b, ln_ffn_g, ln_ffn_b):
    B, S, D = x.shape
    depth = ffn_w_gate.shape[0]
    alpha = (2.0 * depth) ** 0.25
    h = x.reshape(B * S, D)
    hb = h
    for i in range(depth):
        m = i % 4
        if m == 0:
            o = _mla_mixer(h, hb, B, S, mla_w_in, mla_q_norm, mla_w_q_up, mla_kv_norm, mla_w_kv_up)
            w_o = mla_w_o
        elif m == 1:
            o = _gqa_mixer(h, hb, B, S, gqa_w_qkv, gqa_q_norm, gqa_k_norm)
            w_o = gqa_w_o
        elif m == 2:
            o = _na_mixer(h, hb, B, S, na_w_qkv, na_rpb)
            w_o = na_w_o
        else:
            lam_init = 0.8 - 0.6 * math.exp(-0.3 * i)
            o = _diff_mixer(h, hb, B, S, diff_w_qkv,
                            (diff_lambda_q1, diff_lambda_k1, diff_lambda_q2, diff_lambda_k2),
                            diff_subln, lam_init)
            w_o = diff_w_o
        h, hb = _mm_ln(o, w_o.astype(BF16), h, ln_mix_g[i], ln_mix_b[i], alpha=alpha)
        hid = _ffn_up(hb, ffn_w_gate[i].astype(BF16), ffn_w_up[i].astype(BF16))
        h, hb = _mm_ln(hid, ffn_w_down[i].astype(BF16), h, ln_ffn_g[i], ln_ffn_b[i], alpha=alpha)
    return h.reshape(B, S, D)
```

```python
import functools
import math

import jax
import jax.numpy as jnp
from jax import lax
from jax.experimental import pallas as pl
from jax.experimental.pallas import tpu as pltpu

F32 = jnp.float32
BF16 = jnp.bfloat16

LANE = 128
VMEM_LIMIT_BYTES = 56 * 1024 * 1024

GRID_W = 64
Q_BLOCK = 128
ROPE_THETA = 10000.0
NEG_INF = -1e30
MLA_NOPE, MLA_ROPE, MLA_V = 128, 64, 128
GQA_KV_HEADS = 4
NA_KH_MAX, NA_KW = 8, 16
NA_WIN_BLOCKS = 5
DIFF_HD = 128
LN_EPS = 1e-5
RMS_EPS = 1e-6
LOG2E = math.log2(math.e)
FLASH_BODY_KEYS = 16384
PROJ_ROW_BLOCK = 256
DIFF_BODY_KEYS = 8192


def _cparams(sem, **kw):
    return pltpu.CompilerParams(dimension_semantics=sem, vmem_limit_bytes=VMEM_LIMIT_BYTES, **kw)


def _proj_kernel(*refs, n_groups, norm, rope, scale):
    it = iter(refs)
    x_ref, w_ref = next(it), next(it)
    g_ref = next(it) if norm is not None else None
    tabs = (next(it), next(it), next(it)) if rope is not None else None
    o_ref = next(it)
    bm = x_ref.shape[0]
    rb = min(bm, PROJ_ROW_BLOCK)
    for r in range(bm // rb):
        rows = slice(r * rb, (r + 1) * rb)
        acc = jnp.dot(x_ref[rows, :].astype(BF16), w_ref[...], preferred_element_type=F32)
        if norm == "full":
            ms = jnp.mean(acc * acc, axis=-1, keepdims=True)
            acc = acc * lax.rsqrt(ms + RMS_EPS) * g_ref[...]
        for g in range(n_groups):
            a = acc[:, g * LANE:(g + 1) * LANE]
            if norm == "group":
                ms = jnp.mean(a * a, axis=-1, keepdims=True)
                a = a * lax.rsqrt(ms + RMS_EPS) * g_ref[...]
            if rope == "all" or (rope == "odd" and g % 2 == 1):
                c, s1, s2 = tabs[0][rows, :], tabs[1][rows, :], tabs[2][rows, :]
                a = a * c + pltpu.roll(a, 96, 1) * s1 + pltpu.roll(a, 32, 1) * s2
            if scale != 1.0:
                a = a * scale
            o_ref[rows, g * LANE:(g + 1) * LANE] = a.astype(o_ref.dtype)


def _proj(x, w, *, seq, norm=None, gain=None, rope=None, tabs=None, scale=1.0, bm=1024, bn=None):
    T, K = x.shape
    N = w.shape[1]
    if bn is None:
        cap = 1024 if K > 512 else 2048
        bn = N if N <= cap else cap
    bm = min(bm, seq)
    assert T % bm == 0 and seq % bm == 0 and N % bn == 0 and bn % LANE == 0
    if norm == "full":
        assert bn == N
    if rope == "odd":
        assert (bn // LANE) % 2 == 0
    in_specs = [pl.BlockSpec((bm, K), lambda i, j: (i, 0)),
                pl.BlockSpec((K, bn), lambda i, j: (0, j))]
    args = [x, w]
    if norm == "full":
        in_specs.append(pl.BlockSpec((1, bn), lambda i, j: (0, 0)))
        args.append(gain.reshape(1, N).astype(F32))
    elif norm == "group":
        in_specs.append(pl.BlockSpec((1, LANE), lambda i, j: (0, 0)))
        args.append(gain.reshape(1, LANE).astype(F32))
    if rope is not None:
        nseq = seq // bm
        for t in tabs:
            in_specs.append(pl.BlockSpec((bm, LANE), lambda i, j: (i % nseq, 0)))
            args.append(t)
    kern = functools.partial(_proj_kernel, n_groups=bn // LANE, norm=norm, rope=rope, scale=scale)
    return pl.pallas_call(
        kern,
        grid=(T // bm, N // bn),
        in_specs=in_specs,
        out_specs=pl.BlockSpec((bm, bn), lambda i, j: (i, j)),
        out_shape=jax.ShapeDtypeStruct((T, N), BF16),
        compiler_params=_cparams(("parallel", "parallel")),
        name="proj",
    )(*args)


def _mla_down_kernel(x_ref, w_ref, gq_ref, gkv_ref, c_ref, s1_ref, s2_ref, cq_ref, ckv_ref, kr_ref):
    bm = x_ref.shape[0]
    rb = min(bm, PROJ_ROW_BLOCK)
    nq, nkv = cq_ref.shape[1], ckv_ref.shape[1]
    for r in range(bm // rb):
        rows = slice(r * rb, (r + 1) * rb)
        acc = jnp.dot(x_ref[rows, :].astype(BF16), w_ref[...], preferred_element_type=F32)
        for lo, n, g_ref, o_ref in ((0, nq, gq_ref, cq_ref), (nq, nkv, gkv_ref, ckv_ref)):
            a = acc[:, lo:lo + n]
            ms = jnp.mean(a * a, axis=-1, keepdims=True)
            o_ref[rows, :] = (a * lax.rsqrt(ms + RMS_EPS) * g_ref[...]).astype(o_ref.dtype)
        a = acc[:, nq + nkv:]
        a = a * c_ref[rows, :] + pltpu.roll(a, 96, 1) * s1_ref[rows, :] + pltpu.roll(a, 32, 1) * s2_ref[rows, :]
        kr_ref[rows, :] = a.astype(kr_ref.dtype)


def _mla_down(x, w, gq, gkv, tabs, *, seq, bm=1024):
    T, D = x.shape
    nq, nkv = gq.shape[0], gkv.shape[0]
    N = w.shape[1]
    assert N == nq + nkv + LANE and nq % LANE == 0 and nkv % LANE == 0
    bm = min(bm, seq)
    assert T % bm == 0 and seq % bm == 0
    nseq = seq // bm
    tab_spec = pl.BlockSpec((bm, LANE), lambda i: (i % nseq, 0))
    return pl.pallas_call(
        _mla_down_kernel,
        grid=(T // bm,),
        in_specs=[pl.BlockSpec((bm, D), lambda i: (i, 0)),
                  pl.BlockSpec((D, N), lambda i: (0, 0)),
                  pl.BlockSpec((1, nq), lambda i: (0, 0)),
                  pl.BlockSpec((1, nkv), lambda i: (0, 0)),
                  tab_spec, tab_spec, tab_spec],
        out_specs=[pl.BlockSpec((bm, nq), lambda i: (i, 0)),
                   pl.BlockSpec((bm, nkv), lambda i: (i, 0)),
                   pl.BlockSpec((bm, LANE), lambda i: (i, 0))],
        out_shape=[jax.ShapeDtypeStruct((T, nq), BF16), jax.ShapeDtypeStruct((T, nkv), BF16),
                   jax.ShapeDtypeStruct((T, LANE), BF16)],
        compiler_params=_cparams(("parallel",)),
        name="mla_down",
    )(x, w, gq.reshape(1, nq).astype(F32), gkv.reshape(1, nkv).astype(F32), *tabs)


def _mm_ln_kernel(a_ref, w_ref, h_ref, g_ref, b_ref, of_ref, ob_ref, *, alpha):
    y = alpha * h_ref[...] + jnp.dot(a_ref[...], w_ref[...], preferred_element_type=F32)
    mu = jnp.mean(y, axis=-1, keepdims=True)
    d = y - mu
    var = jnp.mean(d * d, axis=-1, keepdims=True)
    out = d * lax.rsqrt(var + LN_EPS) * g_ref[...] + b_ref[...]
    of_ref[...] = out
    ob_ref[...] = out.astype(BF16)


def _mm_ln(a, w, h, g, b, *, alpha, layer=None):
    T, K = a.shape
    D = w.shape[-1]
    if layer is None:
        w_spec = pl.BlockSpec((K, D), lambda i: (0, 0), pipeline_mode=pl.Buffered(1))
    else:
        w_spec = pl.BlockSpec((None, K, D), lambda i: (layer, 0, 0), pipeline_mode=pl.Buffered(1))
    row_bytes = 2 * (2 * K) + 2 * (4 * D) + 2 * (4 * D) + 2 * (2 * D) + 4 * D
    budget = VMEM_LIMIT_BYTES - 2 * K * D - (8 << 20)
    bm = 512
    while bm > 8 and (bm * row_bytes > budget or T % bm):
        bm //= 2
    assert T % bm == 0
    return pl.pallas_call(
        functools.partial(_mm_ln_kernel, alpha=alpha),
        grid=(T // bm,),
        in_specs=[pl.BlockSpec((bm, K), lambda i: (i, 0)),
                  w_spec,
                  pl.BlockSpec((bm, D), lambda i: (i, 0)),
                  pl.BlockSpec((1, D), lambda i: (0, 0)),
                  pl.BlockSpec((1, D), lambda i: (0, 0))],
        out_specs=[pl.BlockSpec((bm, D), lambda i: (i, 0)),
                   pl.BlockSpec((bm, D), lambda i: (i, 0))],
        out_shape=[jax.ShapeDtypeStruct((T, D), F32), jax.ShapeDtypeStruct((T, D), BF16)],
        compiler_params=_cparams(("parallel",)),
        name="mm_ln",
    )(a, w, h, g.reshape(1, D).astype(F32), b.reshape(1, D).astype(F32))


def _ffn_up_kernel(x_ref, wg_ref, wu_ref, o_ref):
    x = x_ref[...]
    g = jnp.dot(x, wg_ref[...], preferred_element_type=F32)
    u = jnp.dot(x, wu_ref[...], preferred_element_type=F32)
    o_ref[...] = ((g / (1.0 + jnp.exp(-g))) * u).astype(o_ref.dtype)


def _ffn_up(x, wg, wu, layer, *, bm=1024, bn=512):
    T, D = x.shape
    Fh = wg.shape[-1]
    assert T % bm == 0 and Fh % bn == 0
    w_spec = pl.BlockSpec((None, D, bn), lambda i, j: (layer, 0, j))
    return pl.pallas_call(
        _ffn_up_kernel,
        grid=(T // bm, Fh // bn),
        in_specs=[pl.BlockSpec((bm, D), lambda i, j: (i, 0)), w_spec, w_spec],
        out_specs=pl.BlockSpec((bm, bn), lambda i, j: (i, j)),
        out_shape=jax.ShapeDtypeStruct((T, Fh), BF16),
        compiler_params=_cparams(("parallel", "parallel")),
        name="ffn_up",
    )(x, wg, wu)


def _chunk_start(c, tk):
    return c * tk if isinstance(c, int) else pl.multiple_of(c * tk, tk)


def _largest_divisor_leq(n, cap):
    return max(d for d in range(1, max(1, min(n, cap)) + 1) if n % d == 0)


def _flash_loop(score_fns, v_ref, s_refs, *, M, tk, body_keys=FLASH_BODY_KEYS):
    S, Dv = v_ref.shape
    n = S // tk
    assert n % 2 == 0
    npairs = n // 2

    def absorb(c, s_ref, mx_cst, carry):
        m, l, acc = carry
        mx, cst = mx_cst
        off = _chunk_start(c, tk)
        m_new = jnp.maximum(m, mx if cst is None else mx + cst)
        a = jnp.exp2(m - m_new)
        p = jnp.exp2(s_ref[...] - (m_new if cst is None else m_new - cst))
        l = a * l + jnp.sum(p, axis=0, keepdims=True)
        pv = lax.dot_general(v_ref[pl.ds(off, tk), :], p.astype(BF16), (((0,), (0,)), ((), ())),
                             preferred_element_type=F32)
        return m_new, l, a * acc + pv

    def pair(scores, issue_next, c0, mx0, state):
        mx1 = scores(c0 + 1, s_refs[1])
        state = absorb(c0, s_refs[0], mx0, state)
        mx0 = issue_next()
        state = absorb(c0 + 1, s_refs[1], mx1, state)
        return mx0, state

    init = (jnp.full((1, M), -jnp.inf, F32), jnp.zeros((1, M), F32), jnp.zeros((Dv, M), F32))
    out = []
    mx0 = score_fns[0](0, s_refs[0])
    for t, scores in enumerate(score_fns):
        state = init
        if npairs > 1:
            def body(jj, carry, scores=scores):
                return pair(scores, lambda: scores(2 * jj + 2, s_refs[0]), 2 * jj, *carry)

            unroll = _largest_divisor_leq(npairs - 1, max(1, body_keys // (2 * tk)))
            mx0, state = lax.fori_loop(0, npairs - 1, body, (mx0, state), unroll=unroll)
        if t + 1 < len(score_fns):
            issue_next = functools.partial(score_fns[t + 1], 0, s_refs[0])
        else:
            issue_next = lambda: None
        mx0, state = pair(scores, issue_next, 2 * (npairs - 1), mx0, state)
        out.append(state[1:])
    return out


def _flash_body(q_tiles, k_ref, v_ref, s_refs, *, tk):
    def make_scores(q):
        qt = q.astype(F32).T.astype(BF16)

        def scores(c, s_ref):
            st = jnp.dot(k_ref[pl.ds(_chunk_start(c, tk), tk), :], qt, preferred_element_type=F32)
            s_ref[...] = st
            return jnp.max(st, axis=0, keepdims=True), None

        return scores

    res = _flash_loop([make_scores(q) for q in q_tiles], v_ref, s_refs, M=q_tiles[0].shape[0], tk=tk)
    return [(acc * (1.0 / l)).T for l, acc in res]


def _gqa_attn_kernel(q_ref, k_ref, v_ref, o_ref, s0_ref, s1_ref, *, G, tq, tk):
    tiles = [slice(r, r + tq) for r in range(0, q_ref.shape[0], tq)]
    qs = [jnp.concatenate([q_ref[rows, g * LANE:(g + 1) * LANE] for g in range(G)], axis=0) for rows in tiles]
    for rows, o in zip(tiles, _flash_body(qs, k_ref, v_ref, (s0_ref, s1_ref), tk=tk)):
        for g in range(G):
            o_ref[rows, g * LANE:(g + 1) * LANE] = o[g * tq:(g + 1) * tq].astype(o_ref.dtype)


def _gqa_attn(q, k, v, *, B, S, tq=128, tk=1024, nt=2):
    H, Hk = q.shape[1] // LANE, k.shape[1] // LANE
    G = H // Hk
    tk = min(tk, S // 2)
    bq = nt * tq
    assert S % bq == 0 and S % tk == 0
    q3, k3, v3 = (t.reshape(B, S, t.shape[1]) for t in (q, k, v))
    kern = functools.partial(_gqa_attn_kernel, G=G, tq=tq, tk=tk)
    out = pl.pallas_call(
        kern,
        grid=(B, Hk, S // bq),
        in_specs=[pl.BlockSpec((None, bq, G * LANE), lambda b, h, i: (b, i, h)),
                  pl.BlockSpec((None, S, LANE), lambda b, h, i: (b, 0, h)),
                  pl.BlockSpec((None, S, LANE), lambda b, h, i: (b, 0, h))],
        out_specs=pl.BlockSpec((None, bq, G * LANE), lambda b, h, i: (b, i, h)),
        out_shape=jax.ShapeDtypeStruct((B, S, H * LANE), BF16),
        scratch_shapes=[pltpu.VMEM((tk, G * tq), F32)] * 2,
        compiler_params=_cparams(("parallel", "parallel", "parallel")),
        name="gqa_attn",
    )(q3, k3, v3)
    return out.reshape(B * S, H * LANE)


def _mla_attn_kernel(q_ref, kn_ref, kr_ref, v_ref, o_ref, kc_ref, s0_ref, s1_ref, *, tq, tk):
    @pl.when(pl.program_id(2) == 0)
    def _():
        kc_ref[:, :LANE] = kn_ref[...]
        kc_ref[:, LANE:] = kr_ref[...]

    tiles = [slice(r, r + tq) for r in range(0, q_ref.shape[0], tq)]
    outs = _flash_body([q_ref[rows, :] for rows in tiles], kc_ref, v_ref, (s0_ref, s1_ref), tk=tk)
    for rows, o in zip(tiles, outs):
        o_ref[rows, :] = o.astype(o_ref.dtype)


def _mla_attn(q, kv, kr, *, B, S, tq=512, tk=1024, nt=2):
    H = q.shape[1] // (2 * LANE)
    tq, tk = min(tq, S // nt), min(tk, S // 2)
    bq = nt * tq
    assert S % bq == 0 and S % tk == 0
    q3, kv3, kr3 = q.reshape(B, S, -1), kv.reshape(B, S, -1), kr.reshape(B, S, LANE)
    kern = functools.partial(_mla_attn_kernel, tq=tq, tk=tk)
    out = pl.pallas_call(
        kern,
        grid=(B, H, S // bq),
        in_specs=[pl.BlockSpec((None, bq, 2 * LANE), lambda b, h, i: (b, i, h)),
                  pl.BlockSpec((None, S, LANE), lambda b, h, i: (b, 0, 2 * h)),
                  pl.BlockSpec((None, S, LANE), lambda b, h, i: (b, 0, 0)),
                  pl.BlockSpec((None, S, LANE), lambda b, h, i: (b, 0, 2 * h + 1))],
        out_specs=pl.BlockSpec((None, bq, LANE), lambda b, h, i: (b, i, h)),
        out_shape=jax.ShapeDtypeStruct((B, S, H * LANE), BF16),
        scratch_shapes=[pltpu.VMEM((S, 2 * LANE), BF16)] + [pltpu.VMEM((tk, tq), F32)] * 2,
        compiler_params=_cparams(("arbitrary", "arbitrary", "arbitrary")),
        name="mla_attn",
    )(q3, kv3, kr3, kv3)
    return out.reshape(B * S, H * LANE)


def _diff_attn_kernel(slopes_ref, q_ref, k_ref, v_ref, lq1_ref, lk1_ref, lq2_ref, lk2_ref, sub_ref, o_ref,
                      s0_ref, s1_ref, t_ref, *, tq, tk, lam_init):
    slope = slopes_ref[pl.program_id(1)]
    arel = slope * (lax.broadcasted_iota(jnp.int32, (tk, tq), 0)
                    - lax.broadcasted_iota(jnp.int32, (tk, tq), 1)).astype(F32)
    t_ref[0] = arel
    t_ref[1] = -arel
    tiles = [slice(r, r + tq) for r in range(0, q_ref.shape[0], tq)]

    def make_scores(t, rows):
        q0 = pl.program_id(2) * q_ref.shape[0] + t * tq
        c_diag = q0 // tk
        t_ref[2 + t] = -jnp.abs(arel + slope * (c_diag * tk - q0).astype(F32))
        qts = [q_ref[rows, c * DIFF_HD:(c + 1) * DIFF_HD].astype(F32).T.astype(BF16) for c in range(2)]

        def scores(c, s_ref):
            off = _chunk_start(c, tk)
            e = slope * (off - q0).astype(F32)
            side = jnp.where(c == c_diag, 2 + t, jnp.where(c > c_diag, 1, 0))
            cst = jnp.where(c == c_diag, 0.0, jnp.where(c > c_diag, -e, e))
            bias = t_ref[side]
            mxs = []
            for comp in range(2):
                ks = k_ref[pl.ds(off, tk), comp * DIFF_HD:(comp + 1) * DIFF_HD]
                st = jnp.dot(ks, qts[comp], preferred_element_type=F32) + bias
                s_ref[:, comp * tq:(comp + 1) * tq] = st
                mxs.append(jnp.max(st, axis=0, keepdims=True))
            return jnp.concatenate(mxs, axis=1), cst

        return scores

    res = _flash_loop([make_scores(t, rows) for t, rows in enumerate(tiles)], v_ref, (s0_ref, s1_ref),
                      M=2 * tq, tk=tk, body_keys=DIFF_BODY_KEYS)
    lam = (jnp.exp(jnp.sum(lq1_ref[...] * lk1_ref[...], axis=-1, keepdims=True))
           - jnp.exp(jnp.sum(lq2_ref[...] * lk2_ref[...], axis=-1, keepdims=True)) + lam_init)
    for rows, (l, acc) in zip(tiles, res):
        on = acc * (1.0 / l)
        o = (on[:, :tq] - lam * on[:, tq:]).T
        ms = jnp.mean(o * o, axis=-1, keepdims=True)
        o = o * lax.rsqrt(ms + RMS_EPS) * sub_ref[...] * (1.0 - lam_init)
        o_ref[rows, :] = o.astype(o_ref.dtype)


def _diff_attn(q, kv, lam_vecs, subln, *, B, S, lam_init, tq=256, tk=512, nt=4):
    W = 2 * DIFF_HD
    H = q.shape[1] // W
    tq, tk = min(tq, S // nt), min(tk, S // 2)
    bq = nt * tq
    assert S % bq == 0 and S % tk == 0 and tk % tq == 0
    slopes = LOG2E * jnp.exp2(-8.0 * jnp.arange(1, H + 1, dtype=F32) / H)
    q3, kv3 = q.reshape(B, S, -1), kv.reshape(B, S, -1)
    vec = pl.BlockSpec((1, DIFF_HD), lambda b, h, i: (0, 0))
    kern = functools.partial(_diff_attn_kernel, tq=tq, tk=tk, lam_init=lam_init)
    out = pl.pallas_call(
        kern,
        grid=(B, H, S // bq),
        in_specs=[pl.BlockSpec(memory_space=pltpu.SMEM),
                  pl.BlockSpec((None, bq, W), lambda b, h, i: (b, i, h)),
                  pl.BlockSpec((None, S, W), lambda b, h, i: (b, 0, h)),
                  pl.BlockSpec((None, S, W), lambda b, h, i: (b, 0, H + h)),
                  vec, vec, vec, vec,
                  pl.BlockSpec((1, W), lambda b, h, i: (0, 0))],
        out_specs=pl.BlockSpec((None, bq, W), lambda b, h, i: (b, i, h)),
        out_shape=jax.ShapeDtypeStruct((B, S, H * W), BF16),
        scratch_shapes=[pltpu.VMEM((tk, 2 * tq), F32)] * 2
                       + [pltpu.VMEM((2 + nt, tk, tq), F32)],
        compiler_params=_cparams(("parallel", "parallel", "parallel")),
        name="diff_attn",
    )(slopes, q3, kv3, kv3, *[v.reshape(1, DIFF_HD).astype(F32) for v in lam_vecs],
      subln.reshape(1, W).astype(F32))
    return out.reshape(B * S, H * W)


def _na_rep_block(p, nblk):
    return jnp.where(p < 3, p, nblk - NA_WIN_BLOCKS + p)


def _na_bias_kernel(rpbT_ref, o_ref, *, nblk, R):
    nk = NA_WIN_BLOCKS * Q_BLOCK
    rows_per_blk = Q_BLOCK // GRID_W
    kh = min(NA_KH_MAX, R)
    j = _na_rep_block(pl.program_id(0), nblk)
    ws = jnp.clip(j - 2, 0, nblk - NA_WIN_BLOCKS)
    tab = rpbT_ref[...]
    n_dr = 2 * NA_KH_MAX - 1
    n_dc = 2 * NA_KW - 1

    w_shift = GRID_W.bit_length() - 1
    n1 = lax.broadcasted_iota(jnp.int32, (32, nk), 1)
    krow1 = rows_per_blk * ws + (n1 >> w_shift)
    halves = []
    for a in range(rows_per_blk):
        qrow = rows_per_blk * j + a
        drow1 = jnp.clip(krow1 - qrow + NA_KH_MAX - 1, 0, n_dr - 1)
        sel = jnp.zeros((32, nk), F32)
        for dr in range(n_dr):
            sel = jnp.where(drow1 == dr, tab[:, dr:dr + 1], sel)
        qc = lax.broadcasted_iota(jnp.int32, (GRID_W, nk), 0)
        n = lax.broadcasted_iota(jnp.int32, (GRID_W, nk), 1)
        kc = n & (GRID_W - 1)
        krow = rows_per_blk * ws + (n >> w_shift)
        rs = jnp.clip(qrow - kh // 2, 0, R - kh)
        cs = jnp.clip(qc - NA_KW // 2, 0, GRID_W - NA_KW)
        ok = (kc >= cs) & (kc < cs + NA_KW) & (krow >= rs) & (krow < rs + kh)
        dcol = jnp.clip(kc - qc + NA_KW - 1, 0, n_dc - 1)
        bias = jnp.zeros((GRID_W, nk), F32)
        for dc in range(n_dc):
            bias = jnp.where(dcol == dc, sel[dc:dc + 1, :], bias)
        halves.append(jnp.where(ok, bias * LOG2E, NEG_INF))
    o_ref[...] = jnp.concatenate(halves, axis=0).T


def _na_bias(rpb, *, S):
    H = rpb.shape[0]
    nblk, R = S // Q_BLOCK, S // GRID_W
    nk = NA_WIN_BLOCKS * Q_BLOCK
    rpbT = jnp.transpose(rpb.astype(F32), (0, 2, 1))
    rpbT = jnp.pad(rpbT, ((0, 0), (0, 32 - rpbT.shape[1]), (0, LANE - rpbT.shape[2])))
    kern = functools.partial(_na_bias_kernel, nblk=nblk, R=R)
    return pl.pallas_call(
        kern,
        grid=(NA_WIN_BLOCKS, H),
        in_specs=[pl.BlockSpec((None, 32, LANE), lambda p, h: (h, 0, 0))],
        out_specs=pl.BlockSpec((None, None, nk, Q_BLOCK), lambda p, h: (p, h, 0, 0)),
        out_shape=jax.ShapeDtypeStruct((NA_WIN_BLOCKS, H, nk, Q_BLOCK), F32),
        compiler_params=_cparams(("parallel", "parallel")),
        name="na_bias",
    )(rpbT)


def _na_attn_kernel(q_ref, *refs, H):
    k_refs = refs[:NA_WIN_BLOCKS]
    v_refs = refs[NA_WIN_BLOCKS:2 * NA_WIN_BLOCKS]
    bias_ref, o_ref = refs[2 * NA_WIN_BLOCKS], refs[2 * NA_WIN_BLOCKS + 1]
    for h in range(H):
        hs = slice(h * LANE, (h + 1) * LANE)
        k = jnp.concatenate([r[:, hs] for r in k_refs], axis=0)
        v = jnp.concatenate([r[:, hs] for r in v_refs], axis=0)
        st = lax.dot_general(k, q_ref[:, hs], (((1,), (1,)), ((), ())), preferred_element_type=F32)
        st = st + bias_ref[h]
        m = jnp.max(st, axis=0, keepdims=True)
        p = jnp.exp2(st - m)
        l = jnp.sum(p, axis=0, keepdims=True)
        ot = lax.dot_general(v, p.astype(BF16), (((0,), (0,)), ((), ())), preferred_element_type=F32)
        o_ref[:, hs] = (ot * (1.0 / l)).T.astype(o_ref.dtype)


def _na_attn(q, kv, bias, *, B, S):
    D = q.shape[1]
    H = D // LANE
    nblk = S // Q_BLOCK
    assert S % Q_BLOCK == 0 and nblk >= NA_WIN_BLOCKS
    q3, kv3 = q.reshape(B, S, D), kv.reshape(B, S, 2 * D)

    def win(j):
        return jnp.clip(j - 2, 0, nblk - NA_WIN_BLOCKS)

    def pattern(j):
        return jnp.where(j < 2, j, jnp.where(j <= nblk - 3, 2, j - (nblk - NA_WIN_BLOCKS)))

    k_specs = [pl.BlockSpec((None, Q_BLOCK, D), lambda b, j, t=t: (b, win(j) + t, 0))
               for t in range(NA_WIN_BLOCKS)]
    v_specs = [pl.BlockSpec((None, Q_BLOCK, D), lambda b, j, t=t: (b, win(j) + t, 1))
               for t in range(NA_WIN_BLOCKS)]
    nk = NA_WIN_BLOCKS * Q_BLOCK
    out = pl.pallas_call(
        functools.partial(_na_attn_kernel, H=H),
        grid=(B, nblk),
        in_specs=[pl.BlockSpec((None, Q_BLOCK, D), lambda b, j: (b, j, 0))] + k_specs + v_specs
                 + [pl.BlockSpec((None, H, nk, Q_BLOCK), lambda b, j: (pattern(j), 0, 0, 0))],
        out_specs=pl.BlockSpec((None, Q_BLOCK, D), lambda b, j: (b, j, 0)),
        out_shape=jax.ShapeDtypeStruct((B, S, D), BF16),
        compiler_params=_cparams(("parallel", "parallel")),
        name="na_attn",
    )(q3, *([kv3] * (2 * NA_WIN_BLOCKS)), bias)
    return out.reshape(B * S, D)


def _rope_cos_sin(pos, dim):
    inv = 1.0 / (ROPE_THETA ** (jnp.arange(0, dim, 2, dtype=F32) / dim))
    ang = pos.astype(F32)[:, None] * inv[None, :]
    return jnp.cos(ang), jnp.sin(ang)


def _rope_tables(pairs, S):
    z = jnp.zeros((S, 32), F32)
    c_parts, s1_parts, s2_parts = [], [], []
    for cos, sin in pairs:
        c_parts += [cos, cos]
        s1_parts += [-sin, z]
        s2_parts += [z, sin]
    pad = LANE - 64 * len(pairs)
    if pad:
        zp = jnp.zeros((S, pad), F32)
        c_parts.append(zp), s1_parts.append(zp), s2_parts.append(zp)
    return tuple(jnp.concatenate(p, axis=1) for p in (c_parts, s1_parts, s2_parts))


def _mla_mixer(h, hb, B, S, w_in, q_norm, w_q_up, kv_norm, w_kv_up):
    QR, KVR = q_norm.shape[0], kv_norm.shape[0]
    H = w_q_up.shape[1] // (MLA_NOPE + MLA_ROPE)
    tabs = _rope_tables([_rope_cos_sin(jnp.arange(S), MLA_ROPE)], S)
    w_in = jnp.pad(w_in.astype(BF16), ((0, 0), (0, LANE - MLA_ROPE)))
    cq, ckv, kr = _mla_down(hb, w_in, q_norm, kv_norm, tabs, seq=S)
    wq = w_q_up.astype(BF16).reshape(QR, H, MLA_NOPE + MLA_ROPE)
    wq = jnp.pad(wq, ((0, 0), (0, 0), (0, 2 * LANE - MLA_NOPE - MLA_ROPE))).reshape(QR, H * 2 * LANE)
    q = _proj(cq, wq, seq=S, rope="odd", tabs=tabs, scale=LOG2E * (MLA_NOPE + MLA_ROPE) ** -0.5)
    kv = _proj(ckv, w_kv_up.astype(BF16), seq=S)
    return _mla_attn(q, kv, kr, B=B, S=S)


def _gqa_mixer(h, hb, B, S, w_qkv, q_norm, k_norm):
    HD = q_norm.shape[0]
    Hk = GQA_KV_HEADS
    nq = w_qkv.shape[1] - 2 * Hk * HD
    half = HD // 2
    pos = jnp.arange(S)
    tabs = _rope_tables([_rope_cos_sin(pos // GRID_W, half), _rope_cos_sin(pos % GRID_W, half)], S)
    w = w_qkv.astype(BF16)
    q = _proj(hb, w[:, :nq], seq=S, norm="group", gain=q_norm, rope="all", tabs=tabs, scale=LOG2E * HD ** -0.5)
    k = _proj(hb, w[:, nq:nq + Hk * HD], seq=S, norm="group", gain=k_norm, rope="all", tabs=tabs)
    v = _proj(hb, w[:, nq + Hk * HD:], seq=S)
    return _gqa_attn(q, k, v, B=B, S=S)


def _na_mixer(h, hb, B, S, w_qkv, rpb):
    D = w_qkv.shape[1] // 3
    w = w_qkv.astype(BF16)
    q = _proj(hb, w[:, :D], seq=S, scale=LOG2E * LANE ** -0.5)
    kv = _proj(hb, w[:, D:], seq=S)
    return _na_attn(q, kv, _na_bias(rpb, S=S), B=B, S=S)


def _diff_mixer(h, hb, B, S, w_qkv, lam_vecs, subln, lam_init):
    D = w_qkv.shape[1] // 3
    w = w_qkv.astype(BF16)
    q = _proj(hb, w[:, :D], seq=S, scale=LOG2E * DIFF_HD ** -0.5)
    kv = _proj(hb, w[:, D:], seq=S)
    return _diff_attn(q, kv, lam_vecs, subln, B=B, S=S, lam_init=lam_init)


def kernel(x, mla_w_in, mla_q_norm, mla_w_q_up, mla_kv_norm, mla_w_kv_up, mla_w_o, gqa_w_qkv, gqa_q_norm, gqa_k_norm, gqa_w_o, na_w_qkv, na_rpb, na_w_o, diff_w_qkv, diff_lambda_q1, diff_lambda_k1, diff_lambda_q2, diff_lambda_k2, diff_subln, diff_w_o, ffn_w_gate, ffn_w_up, ffn_w_down, ln_mix_g, ln_mix_b, ln_ffn_g, ln_ffn_b):
    B, S, D = x.shape
    depth = ffn_w_gate.shape[0]
    alpha = (2.0 * depth) ** 0.25
    h = x.reshape(B * S, D)
    hb = h
    wg_all, wu_all, wd_all = (w.astype(BF16) for w in (ffn_w_gate, ffn_w_up, ffn_w_down))
    for i in range(depth):
        m = i % 4
        if m == 0:
            o = _mla_mixer(h, hb, B, S, mla_w_in, mla_q_norm, mla_w_q_up, mla_kv_norm, mla_w_kv_up)
            w_o = mla_w_o
        elif m == 1:
            o = _gqa_mixer(h, hb, B, S, gqa_w_qkv, gqa_q_norm, gqa_k_norm)
            w_o = gqa_w_o
        elif m == 2:
            o = _na_mixer(h, hb, B, S, na_w_qkv, na_rpb)
            w_o = na_w_o
        else:
            lam_init = 0.8 - 0.6 * math.exp(-0.3 * i)
            o = _diff_mixer(h, hb, B, S, diff_w_qkv,
                            (diff_lambda_q1, diff_lambda_k1, diff_lambda_q2, diff_lambda_k2),
                            diff_subln, lam_init)
            w_o = diff_w_o
        h, hb = _mm_ln(o, w_o.astype(BF16), h, ln_mix_g[i], ln_mix_b[i], alpha=alpha)
        hid = _ffn_up(hb, wg_all, wu_all, i)
        h, hb = _mm_ln(hid, wd_all, h, ln_ffn_g[i], ln_ffn_b[i], alpha=alpha, layer=i)
    return h.reshape(B, S, D)
```

```python
import functools
import math

import jax
import jax.numpy as jnp
from jax import lax
from jax.experimental import pallas as pl
from jax.experimental.pallas import tpu as pltpu

F32 = jnp.float32
BF16 = jnp.bfloat16

LANE = 128
VMEM_LIMIT_BYTES = 56 * 1024 * 1024
VMEM_SLACK_BYTES = 8 * 1024 * 1024
ROT_HALF = 32

GRID_W = 64
Q_BLOCK = 128
ROPE_THETA = 10000.0
NEG_INF = -1e30
MLA_NOPE, MLA_ROPE, MLA_V = 128, 64, 128
GQA_KV_HEADS = 4
NA_KH_MAX, NA_KW = 8, 16
NA_WIN_BLOCKS = 5
NA_DC_PAD = 32
DIFF_HD = 128
LN_EPS = 1e-5
RMS_EPS = 1e-6
LOG2E = math.log2(math.e)
FLASH_BODY_KEYS = 16384
PROJ_ROW_BLOCK = 256
DIFF_BODY_KEYS = 8192


def _cparams(sem, **kw):
    return pltpu.CompilerParams(dimension_semantics=sem, vmem_limit_bytes=VMEM_LIMIT_BYTES, **kw)


def _rotate_pairs(a, c, s1, s2):
    x2_under_x1 = pltpu.roll(a, LANE - ROT_HALF, 1)
    x1_under_x2 = pltpu.roll(a, ROT_HALF, 1)
    return a * c + x2_under_x1 * s1 + x1_under_x2 * s2


def _proj_kernel(*refs, n_groups, norm, rope, scale):
    it = iter(refs)
    x_ref, w_ref = next(it), next(it)
    g_ref = next(it) if norm is not None else None
    tabs = (next(it), next(it), next(it)) if rope is not None else None
    o_ref = next(it)
    bm = x_ref.shape[0]
    rb = min(bm, PROJ_ROW_BLOCK)
    for r in range(bm // rb):
        rows = slice(r * rb, (r + 1) * rb)
        acc = jnp.dot(x_ref[rows, :].astype(BF16), w_ref[...], preferred_element_type=F32)
        if norm == "full":
            ms = jnp.mean(acc * acc, axis=-1, keepdims=True)
            acc = acc * lax.rsqrt(ms + RMS_EPS) * g_ref[...]
        for g in range(n_groups):
            a = acc[:, g * LANE:(g + 1) * LANE]
            if norm == "group":
                ms = jnp.mean(a * a, axis=-1, keepdims=True)
                a = a * lax.rsqrt(ms + RMS_EPS) * g_ref[...]
            if rope == "all" or (rope == "odd" and g % 2 == 1):
                a = _rotate_pairs(a, tabs[0][rows, :], tabs[1][rows, :], tabs[2][rows, :])
            if scale != 1.0:
                a = a * scale
            o_ref[rows, g * LANE:(g + 1) * LANE] = a.astype(o_ref.dtype)


def _proj(x, w, *, seq, norm=None, gain=None, rope=None, tabs=None, scale=1.0, bm=1024, bn=None):
    T, K = x.shape
    N = w.shape[1]
    if bn is None:
        cap = 1024 if K > 512 else 2048
        bn = N if N <= cap else cap
    bm = min(bm, seq)
    assert T % bm == 0 and seq % bm == 0 and N % bn == 0 and bn % LANE == 0
    if norm == "full":
        assert bn == N
    if rope == "odd":
        assert (bn // LANE) % 2 == 0
    in_specs = [pl.BlockSpec((bm, K), lambda i, j: (i, 0)),
                pl.BlockSpec((K, bn), lambda i, j: (0, j))]
    args = [x, w]
    if norm == "full":
        in_specs.append(pl.BlockSpec((1, bn), lambda i, j: (0, 0)))
        args.append(gain.reshape(1, N).astype(F32))
    elif norm == "group":
        in_specs.append(pl.BlockSpec((1, LANE), lambda i, j: (0, 0)))
        args.append(gain.reshape(1, LANE).astype(F32))
    if rope is not None:
        nseq = seq // bm
        for t in tabs:
            in_specs.append(pl.BlockSpec((bm, LANE), lambda i, j: (i % nseq, 0)))
            args.append(t)
    kern = functools.partial(_proj_kernel, n_groups=bn // LANE, norm=norm, rope=rope, scale=scale)
    return pl.pallas_call(
        kern,
        grid=(T // bm, N // bn),
        in_specs=in_specs,
        out_specs=pl.BlockSpec((bm, bn), lambda i, j: (i, j)),
        out_shape=jax.ShapeDtypeStruct((T, N), BF16),
        compiler_params=_cparams(("parallel", "parallel")),
        name="proj",
    )(*args)


def _mla_down_kernel(x_ref, w_ref, gq_ref, gkv_ref, c_ref, s1_ref, s2_ref, cq_ref, ckv_ref, kr_ref):
    bm = x_ref.shape[0]
    rb = min(bm, PROJ_ROW_BLOCK)
    nq, nkv = cq_ref.shape[1], ckv_ref.shape[1]
    for r in range(bm // rb):
        rows = slice(r * rb, (r + 1) * rb)
        acc = jnp.dot(x_ref[rows, :].astype(BF16), w_ref[...], preferred_element_type=F32)
        for lo, n, g_ref, o_ref in ((0, nq, gq_ref, cq_ref), (nq, nkv, gkv_ref, ckv_ref)):
            a = acc[:, lo:lo + n]
            ms = jnp.mean(a * a, axis=-1, keepdims=True)
            o_ref[rows, :] = (a * lax.rsqrt(ms + RMS_EPS) * g_ref[...]).astype(o_ref.dtype)
        a = _rotate_pairs(acc[:, nq + nkv:], c_ref[rows, :], s1_ref[rows, :], s2_ref[rows, :])
        kr_ref[rows, :] = a.astype(kr_ref.dtype)


def _mla_down(x, w, gq, gkv, tabs, *, seq, bm=1024):
    T, D = x.shape
    nq, nkv = gq.shape[0], gkv.shape[0]
    N = w.shape[1]
    assert N == nq + nkv + LANE and nq % LANE == 0 and nkv % LANE == 0
    bm = min(bm, seq)
    assert T % bm == 0 and seq % bm == 0
    nseq = seq // bm
    tab_spec = pl.BlockSpec((bm, LANE), lambda i: (i % nseq, 0))
    return pl.pallas_call(
        _mla_down_kernel,
        grid=(T // bm,),
        in_specs=[pl.BlockSpec((bm, D), lambda i: (i, 0)),
                  pl.BlockSpec((D, N), lambda i: (0, 0)),
                  pl.BlockSpec((1, nq), lambda i: (0, 0)),
                  pl.BlockSpec((1, nkv), lambda i: (0, 0)),
                  tab_spec, tab_spec, tab_spec],
        out_specs=[pl.BlockSpec((bm, nq), lambda i: (i, 0)),
                   pl.BlockSpec((bm, nkv), lambda i: (i, 0)),
                   pl.BlockSpec((bm, LANE), lambda i: (i, 0))],
        out_shape=[jax.ShapeDtypeStruct((T, nq), BF16), jax.ShapeDtypeStruct((T, nkv), BF16),
                   jax.ShapeDtypeStruct((T, LANE), BF16)],
        compiler_params=_cparams(("parallel",)),
        name="mla_down",
    )(x, w, gq.reshape(1, nq).astype(F32), gkv.reshape(1, nkv).astype(F32), *tabs)


def _mm_ln_kernel(a_ref, w_ref, h_ref, g_ref, b_ref, of_ref, ob_ref, *, alpha):
    y = alpha * h_ref[...] + jnp.dot(a_ref[...], w_ref[...], preferred_element_type=F32)
    mu = jnp.mean(y, axis=-1, keepdims=True)
    d = y - mu
    var = jnp.mean(d * d, axis=-1, keepdims=True)
    out = d * lax.rsqrt(var + LN_EPS) * g_ref[...] + b_ref[...]
    of_ref[...] = out
    ob_ref[...] = out.astype(BF16)


def _mm_ln(a, w, h, g, b, *, alpha, layer=None):
    T, K = a.shape
    D = w.shape[-1]
    if layer is None:
        w_spec = pl.BlockSpec((K, D), lambda i: (0, 0), pipeline_mode=pl.Buffered(1))
    else:
        w_spec = pl.BlockSpec((None, K, D), lambda i: (layer, 0, 0), pipeline_mode=pl.Buffered(1))
    row_bytes = 2 * (2 * K) + 2 * (4 * D) + 2 * (4 * D) + 2 * (2 * D) + 4 * D
    budget = VMEM_LIMIT_BYTES - 2 * K * D - VMEM_SLACK_BYTES
    bm = 512
    while bm > 8 and (bm * row_bytes > budget or T % bm):
        bm //= 2
    assert T % bm == 0
    return pl.pallas_call(
        functools.partial(_mm_ln_kernel, alpha=alpha),
        grid=(T // bm,),
        in_specs=[pl.BlockSpec((bm, K), lambda i: (i, 0)),
                  w_spec,
                  pl.BlockSpec((bm, D), lambda i: (i, 0)),
                  pl.BlockSpec((1, D), lambda i: (0, 0)),
                  pl.BlockSpec((1, D), lambda i: (0, 0))],
        out_specs=[pl.BlockSpec((bm, D), lambda i: (i, 0)),
                   pl.BlockSpec((bm, D), lambda i: (i, 0))],
        out_shape=[jax.ShapeDtypeStruct((T, D), F32), jax.ShapeDtypeStruct((T, D), BF16)],
        compiler_params=_cparams(("parallel",)),
        name="mm_ln",
    )(a, w, h, g.reshape(1, D).astype(F32), b.reshape(1, D).astype(F32))


def _ffn_up_kernel(x_ref, wg_ref, wu_ref, o_ref):
    x = x_ref[...]
    g = jnp.dot(x, wg_ref[...], preferred_element_type=F32)
    u = jnp.dot(x, wu_ref[...], preferred_element_type=F32)
    o_ref[...] = ((g / (1.0 + jnp.exp(-g))) * u).astype(o_ref.dtype)


def _ffn_up(x, wg, wu, layer, *, bm=1024, bn=512):
    T, D = x.shape
    Fh = wg.shape[-1]
    assert T % bm == 0 and Fh % bn == 0
    w_spec = pl.BlockSpec((None, D, bn), lambda i, j: (layer, 0, j))
    return pl.pallas_call(
        _ffn_up_kernel,
        grid=(T // bm, Fh // bn),
        in_specs=[pl.BlockSpec((bm, D), lambda i, j: (i, 0)), w_spec, w_spec],
        out_specs=pl.BlockSpec((bm, bn), lambda i, j: (i, j)),
        out_shape=jax.ShapeDtypeStruct((T, Fh), BF16),
        compiler_params=_cparams(("parallel", "parallel")),
        name="ffn_up",
    )(x, wg, wu)


def _chunk_start(c, tk):
    return c * tk if isinstance(c, int) else pl.multiple_of(c * tk, tk)


def _largest_divisor_leq(n, cap):
    return max(d for d in range(1, max(1, min(n, cap)) + 1) if n % d == 0)


def _flash_loop(score_fns, v_ref, s_refs, *, M, tk, body_keys=FLASH_BODY_KEYS):
    S, Dv = v_ref.shape
    n = S // tk
    assert n % 2 == 0
    npairs = n // 2

    def absorb(c, s_ref, mx_cst, carry):
        m, l, acc = carry
        mx, cst = mx_cst
        off = _chunk_start(c, tk)
        m_new = jnp.maximum(m, mx if cst is None else mx + cst)
        a = jnp.exp2(m - m_new)
        p = jnp.exp2(s_ref[...] - (m_new if cst is None else m_new - cst))
        l = a * l + jnp.sum(p, axis=0, keepdims=True)
        pv = lax.dot_general(v_ref[pl.ds(off, tk), :], p.astype(BF16), (((0,), (0,)), ((), ())),
                             preferred_element_type=F32)
        return m_new, l, a * acc + pv

    def pair(scores, issue_next, c0, mx0, state):
        mx1 = scores(c0 + 1, s_refs[1])
        state = absorb(c0, s_refs[0], mx0, state)
        mx0 = issue_next()
        state = absorb(c0 + 1, s_refs[1], mx1, state)
        return mx0, state

    init = (jnp.full((1, M), -jnp.inf, F32), jnp.zeros((1, M), F32), jnp.zeros((Dv, M), F32))
    out = []
    mx0 = score_fns[0](0, s_refs[0])
    for t, scores in enumerate(score_fns):
        state = init
        if npairs > 1:
            def body(jj, carry, scores=scores):
                return pair(scores, lambda: scores(2 * jj + 2, s_refs[0]), 2 * jj, *carry)

            unroll = _largest_divisor_leq(npairs - 1, max(1, body_keys // (2 * tk)))
            mx0, state = lax.fori_loop(0, npairs - 1, body, (mx0, state), unroll=unroll)
        if t + 1 < len(score_fns):
            issue_next = functools.partial(score_fns[t + 1], 0, s_refs[0])
        else:
            issue_next = lambda: None
        mx0, state = pair(scores, issue_next, 2 * (npairs - 1), mx0, state)
        out.append(state[1:])
    return out


def _flash_body(q_tiles, k_ref, v_ref, s_refs, *, tk):
    def make_scores(q):
        qt = q.astype(F32).T.astype(BF16)

        def scores(c, s_ref):
            st = jnp.dot(k_ref[pl.ds(_chunk_start(c, tk), tk), :], qt, preferred_element_type=F32)
            s_ref[...] = st
            return jnp.max(st, axis=0, keepdims=True), None

        return scores

    res = _flash_loop([make_scores(q) for q in q_tiles], v_ref, s_refs, M=q_tiles[0].shape[0], tk=tk)
    return [(acc * (1.0 / l)).T for l, acc in res]


def _gqa_attn_kernel(q_ref, k_ref, v_ref, o_ref, s0_ref, s1_ref, *, G, tq, tk):
    tiles = [slice(r, r + tq) for r in range(0, q_ref.shape[0], tq)]
    qs = [jnp.concatenate([q_ref[rows, g * LANE:(g + 1) * LANE] for g in range(G)], axis=0) for rows in tiles]
    for rows, o in zip(tiles, _flash_body(qs, k_ref, v_ref, (s0_ref, s1_ref), tk=tk)):
        for g in range(G):
            o_ref[rows, g * LANE:(g + 1) * LANE] = o[g * tq:(g + 1) * tq].astype(o_ref.dtype)


def _gqa_attn(q, k, v, *, B, S, tq=128, tk=1024, nt=2):
    H, Hk = q.shape[1] // LANE, k.shape[1] // LANE
    G = H // Hk
    tk = min(tk, S // 2)
    bq = nt * tq
    assert S % bq == 0 and S % tk == 0
    q3, k3, v3 = (t.reshape(B, S, t.shape[1]) for t in (q, k, v))
    kern = functools.partial(_gqa_attn_kernel, G=G, tq=tq, tk=tk)
    out = pl.pallas_call(
        kern,
        grid=(B, Hk, S // bq),
        in_specs=[pl.BlockSpec((None, bq, G * LANE), lambda b, h, i: (b, i, h)),
                  pl.BlockSpec((None, S, LANE), lambda b, h, i: (b, 0, h)),
                  pl.BlockSpec((None, S, LANE), lambda b, h, i: (b, 0, h))],
        out_specs=pl.BlockSpec((None, bq, G * LANE), lambda b, h, i: (b, i, h)),
        out_shape=jax.ShapeDtypeStruct((B, S, H * LANE), BF16),
        scratch_shapes=[pltpu.VMEM((tk, G * tq), F32)] * 2,
        compiler_params=_cparams(("parallel", "parallel", "parallel")),
        name="gqa_attn",
    )(q3, k3, v3)
    return out.reshape(B * S, H * LANE)


def _mla_attn_kernel(q_ref, kn_ref, kr_ref, v_ref, o_ref, kc_ref, s0_ref, s1_ref, *, tq, tk):
    @pl.when(pl.program_id(2) == 0)
    def _():
        kc_ref[:, :LANE] = kn_ref[...]
        kc_ref[:, LANE:] = kr_ref[...]

    tiles = [slice(r, r + tq) for r in range(0, q_ref.shape[0], tq)]
    outs = _flash_body([q_ref[rows, :] for rows in tiles], kc_ref, v_ref, (s0_ref, s1_ref), tk=tk)
    for rows, o in zip(tiles, outs):
        o_ref[rows, :] = o.astype(o_ref.dtype)


def _mla_attn(q, kv, kr, *, B, S, tq=512, tk=1024, nt=2):
    H = q.shape[1] // (2 * LANE)
    tq, tk = min(tq, S // nt), min(tk, S // 2)
    bq = nt * tq
    assert S % bq == 0 and S % tk == 0
    q3, kv3, kr3 = q.reshape(B, S, -1), kv.reshape(B, S, -1), kr.reshape(B, S, LANE)
    kern = functools.partial(_mla_attn_kernel, tq=tq, tk=tk)
    out = pl.pallas_call(
        kern,
        grid=(B, H, S // bq),
        in_specs=[pl.BlockSpec((None, bq, 2 * LANE), lambda b, h, i: (b, i, h)),
                  pl.BlockSpec((None, S, LANE), lambda b, h, i: (b, 0, 2 * h)),
                  pl.BlockSpec((None, S, LANE), lambda b, h, i: (b, 0, 0)),
                  pl.BlockSpec((None, S, LANE), lambda b, h, i: (b, 0, 2 * h + 1))],
        out_specs=pl.BlockSpec((None, bq, LANE), lambda b, h, i: (b, i, h)),
        out_shape=jax.ShapeDtypeStruct((B, S, H * LANE), BF16),
        scratch_shapes=[pltpu.VMEM((S, 2 * LANE), BF16)] + [pltpu.VMEM((tk, tq), F32)] * 2,
        compiler_params=_cparams(("arbitrary", "arbitrary", "arbitrary")),
        name="mla_attn",
    )(q3, kv3, kr3, kv3)
    return out.reshape(B * S, H * LANE)


def _diff_attn_kernel(slopes_ref, q_ref, k_ref, v_ref, lq1_ref, lk1_ref, lq2_ref, lk2_ref, sub_ref, o_ref,
                      s0_ref, s1_ref, t_ref, *, tq, tk, lam_init):
    slope = slopes_ref[pl.program_id(1)]
    arel = slope * (lax.broadcasted_iota(jnp.int32, (tk, tq), 0)
                    - lax.broadcasted_iota(jnp.int32, (tk, tq), 1)).astype(F32)
    t_ref[0] = arel
    t_ref[1] = -arel
    tiles = [slice(r, r + tq) for r in range(0, q_ref.shape[0], tq)]

    def make_scores(t, rows):
        q0 = pl.program_id(2) * q_ref.shape[0] + t * tq
        c_diag = q0 // tk
        t_ref[2 + t] = -jnp.abs(arel + slope * (c_diag * tk - q0).astype(F32))
        qts = [q_ref[rows, c * DIFF_HD:(c + 1) * DIFF_HD].astype(F32).T.astype(BF16) for c in range(2)]

        def scores(c, s_ref):
            off = _chunk_start(c, tk)
            e = slope * (off - q0).astype(F32)
            side = jnp.where(c == c_diag, 2 + t, jnp.where(c > c_diag, 1, 0))
            cst = jnp.where(c == c_diag, 0.0, jnp.where(c > c_diag, -e, e))
            bias = t_ref[side]
            mxs = []
            for comp in range(2):
                ks = k_ref[pl.ds(off, tk), comp * DIFF_HD:(comp + 1) * DIFF_HD]
                st = jnp.dot(ks, qts[comp], preferred_element_type=F32) + bias
                s_ref[:, comp * tq:(comp + 1) * tq] = st
                mxs.append(jnp.max(st, axis=0, keepdims=True))
            return jnp.concatenate(mxs, axis=1), cst

        return scores

    res = _flash_loop([make_scores(t, rows) for t, rows in enumerate(tiles)], v_ref, (s0_ref, s1_ref),
                      M=2 * tq, tk=tk, body_keys=DIFF_BODY_KEYS)
    lam = (jnp.exp(jnp.sum(lq1_ref[...] * lk1_ref[...], axis=-1, keepdims=True))
           - jnp.exp(jnp.sum(lq2_ref[...] * lk2_ref[...], axis=-1, keepdims=True)) + lam_init)
    for rows, (l, acc) in zip(tiles, res):
        on = acc * (1.0 / l)
        o = (on[:, :tq] - lam * on[:, tq:]).T
        ms = jnp.mean(o * o, axis=-1, keepdims=True)
        o = o * lax.rsqrt(ms + RMS_EPS) * sub_ref[...] * (1.0 - lam_init)
        o_ref[rows, :] = o.astype(o_ref.dtype)


def _diff_attn(q, kv, lam_vecs, subln, *, B, S, lam_init, tq=256, tk=512, nt=4):
    W = 2 * DIFF_HD
    H = q.shape[1] // W
    tq, tk = min(tq, S // nt), min(tk, S // 2)
    bq = nt * tq
    assert S % bq == 0 and S % tk == 0 and tk % tq == 0
    slopes = LOG2E * jnp.exp2(-8.0 * jnp.arange(1, H + 1, dtype=F32) / H)
    q3, kv3 = q.reshape(B, S, -1), kv.reshape(B, S, -1)
    vec = pl.BlockSpec((1, DIFF_HD), lambda b, h, i: (0, 0))
    kern = functools.partial(_diff_attn_kernel, tq=tq, tk=tk, lam_init=lam_init)
    out = pl.pallas_call(
        kern,
        grid=(B, H, S // bq),
        in_specs=[pl.BlockSpec(memory_space=pltpu.SMEM),
                  pl.BlockSpec((None, bq, W), lambda b, h, i: (b, i, h)),
                  pl.BlockSpec((None, S, W), lambda b, h, i: (b, 0, h)),
                  pl.BlockSpec((None, S, W), lambda b, h, i: (b, 0, H + h)),
                  vec, vec, vec, vec,
                  pl.BlockSpec((1, W), lambda b, h, i: (0, 0))],
        out_specs=pl.BlockSpec((None, bq, W), lambda b, h, i: (b, i, h)),
        out_shape=jax.ShapeDtypeStruct((B, S, H * W), BF16),
        scratch_shapes=[pltpu.VMEM((tk, 2 * tq), F32)] * 2
                       + [pltpu.VMEM((2 + nt, tk, tq), F32)],
        compiler_params=_cparams(("parallel", "parallel", "parallel")),
        name="diff_attn",
    )(slopes, q3, kv3, kv3, *[v.reshape(1, DIFF_HD).astype(F32) for v in lam_vecs],
      subln.reshape(1, W).astype(F32))
    return out.reshape(B * S, H * W)


def _na_rep_block(p, nblk):
    return jnp.where(p < 3, p, nblk - NA_WIN_BLOCKS + p)


def _na_bias_kernel(rpbT_ref, o_ref, *, nblk, R):
    nk = NA_WIN_BLOCKS * Q_BLOCK
    rows_per_blk = Q_BLOCK // GRID_W
    kh = min(NA_KH_MAX, R)
    j = _na_rep_block(pl.program_id(0), nblk)
    ws = jnp.clip(j - 2, 0, nblk - NA_WIN_BLOCKS)
    tab = rpbT_ref[...]
    n_dr = 2 * NA_KH_MAX - 1
    n_dc = 2 * NA_KW - 1

    w_shift = GRID_W.bit_length() - 1
    n1 = lax.broadcasted_iota(jnp.int32, (NA_DC_PAD, nk), 1)
    krow1 = rows_per_blk * ws + (n1 >> w_shift)
    halves = []
    for a in range(rows_per_blk):
        qrow = rows_per_blk * j + a
        drow1 = jnp.clip(krow1 - qrow + NA_KH_MAX - 1, 0, n_dr - 1)
        sel = jnp.zeros((NA_DC_PAD, nk), F32)
        for dr in range(n_dr):
            sel = jnp.where(drow1 == dr, tab[:, dr:dr + 1], sel)
        qc = lax.broadcasted_iota(jnp.int32, (GRID_W, nk), 0)
        n = lax.broadcasted_iota(jnp.int32, (GRID_W, nk), 1)
        kc = n & (GRID_W - 1)
        krow = rows_per_blk * ws + (n >> w_shift)
        rs = jnp.clip(qrow - kh // 2, 0, R - kh)
        cs = jnp.clip(qc - NA_KW // 2, 0, GRID_W - NA_KW)
        ok = (kc >= cs) & (kc < cs + NA_KW) & (krow >= rs) & (krow < rs + kh)
        dcol = jnp.clip(kc - qc + NA_KW - 1, 0, n_dc - 1)
        bias = jnp.zeros((GRID_W, nk), F32)
        for dc in range(n_dc):
            bias = jnp.where(dcol == dc, sel[dc:dc + 1, :], bias)
        halves.append(jnp.where(ok, bias * LOG2E, NEG_INF))
    o_ref[...] = jnp.concatenate(halves, axis=0).T


def _na_bias(rpb, *, S):
    H = rpb.shape[0]
    nblk, R = S // Q_BLOCK, S // GRID_W
    nk = NA_WIN_BLOCKS * Q_BLOCK
    rpbT = jnp.transpose(rpb.astype(F32), (0, 2, 1))
    rpbT = jnp.pad(rpbT, ((0, 0), (0, NA_DC_PAD - rpbT.shape[1]), (0, LANE - rpbT.shape[2])))
    kern = functools.partial(_na_bias_kernel, nblk=nblk, R=R)
    return pl.pallas_call(
        kern,
        grid=(NA_WIN_BLOCKS, H),
        in_specs=[pl.BlockSpec((None, NA_DC_PAD, LANE), lambda p, h: (h, 0, 0))],
        out_specs=pl.BlockSpec((None, None, nk, Q_BLOCK), lambda p, h: (p, h, 0, 0)),
        out_shape=jax.ShapeDtypeStruct((NA_WIN_BLOCKS, H, nk, Q_BLOCK), F32),
        compiler_params=_cparams(("parallel", "parallel")),
        name="na_bias",
    )(rpbT)


def _na_attn_kernel(q_ref, *refs, H):
    k_refs = refs[:NA_WIN_BLOCKS]
    v_refs = refs[NA_WIN_BLOCKS:2 * NA_WIN_BLOCKS]
    bias_ref, o_ref = refs[2 * NA_WIN_BLOCKS], refs[2 * NA_WIN_BLOCKS + 1]
    for h in range(H):
        hs = slice(h * LANE, (h + 1) * LANE)
        k = jnp.concatenate([r[:, hs] for r in k_refs], axis=0)
        v = jnp.concatenate([r[:, hs] for r in v_refs], axis=0)
        st = lax.dot_general(k, q_ref[:, hs], (((1,), (1,)), ((), ())), preferred_element_type=F32)
        st = st + bias_ref[h]
        m = jnp.max(st, axis=0, keepdims=True)
        p = jnp.exp2(st - m)
        l = jnp.sum(p, axis=0, keepdims=True)
        ot = lax.dot_general(v, p.astype(BF16), (((0,), (0,)), ((), ())), preferred_element_type=F32)
        o_ref[:, hs] = (ot * (1.0 / l)).T.astype(o_ref.dtype)


def _na_attn(q, kv, bias, *, B, S):
    D = q.shape[1]
    H = D // LANE
    nblk = S // Q_BLOCK
    assert S % Q_BLOCK == 0 and nblk >= NA_WIN_BLOCKS
    q3, kv3 = q.reshape(B, S, D), kv.reshape(B, S, 2 * D)

    def win(j):
        return jnp.clip(j - 2, 0, nblk - NA_WIN_BLOCKS)

    def pattern(j):
        return jnp.where(j < 2, j, jnp.where(j <= nblk - 3, 2, j - (nblk - NA_WIN_BLOCKS)))

    k_specs = [pl.BlockSpec((None, Q_BLOCK, D), lambda b, j, t=t: (b, win(j) + t, 0))
               for t in range(NA_WIN_BLOCKS)]
    v_specs = [pl.BlockSpec((None, Q_BLOCK, D), lambda b, j, t=t: (b, win(j) + t, 1))
               for t in range(NA_WIN_BLOCKS)]
    nk = NA_WIN_BLOCKS * Q_BLOCK
    out = pl.pallas_call(
        functools.partial(_na_attn_kernel, H=H),
        grid=(B, nblk),
        in_specs=[pl.BlockSpec((None, Q_BLOCK, D), lambda b, j: (b, j, 0))] + k_specs + v_specs
                 + [pl.BlockSpec((None, H, nk, Q_BLOCK), lambda b, j: (pattern(j), 0, 0, 0))],
        out_specs=pl.BlockSpec((None, Q_BLOCK, D), lambda b, j: (b, j, 0)),
        out_shape=jax.ShapeDtypeStruct((B, S, D), BF16),
        compiler_params=_cparams(("parallel", "parallel")),
        name="na_attn",
    )(q3, *([kv3] * (2 * NA_WIN_BLOCKS)), bias)
    return out.reshape(B * S, D)


def _rope_cos_sin(pos, dim):
    inv = 1.0 / (ROPE_THETA ** (jnp.arange(0, dim, 2, dtype=F32) / dim))
    ang = pos.astype(F32)[:, None] * inv[None, :]
    return jnp.cos(ang), jnp.sin(ang)


def _rope_tables(pairs, S):
    z = jnp.zeros((S, ROT_HALF), F32)
    c_parts, s1_parts, s2_parts = [], [], []
    for cos, sin in pairs:
        c_parts += [cos, cos]
        s1_parts += [-sin, z]
        s2_parts += [z, sin]
    pad = LANE - 2 * ROT_HALF * len(pairs)
    if pad:
        zp = jnp.zeros((S, pad), F32)
        c_parts.append(zp), s1_parts.append(zp), s2_parts.append(zp)
    return tuple(jnp.concatenate(p, axis=1) for p in (c_parts, s1_parts, s2_parts))


def _mla_mixer(h, hb, B, S, w_in, q_norm, w_q_up, kv_norm, w_kv_up):
    QR, KVR = q_norm.shape[0], kv_norm.shape[0]
    H = w_q_up.shape[1] // (MLA_NOPE + MLA_ROPE)
    tabs = _rope_tables([_rope_cos_sin(jnp.arange(S), MLA_ROPE)], S)
    w_in = jnp.pad(w_in.astype(BF16), ((0, 0), (0, LANE - MLA_ROPE)))
    cq, ckv, kr = _mla_down(hb, w_in, q_norm, kv_norm, tabs, seq=S)
    wq = w_q_up.astype(BF16).reshape(QR, H, MLA_NOPE + MLA_ROPE)
    wq = jnp.pad(wq, ((0, 0), (0, 0), (0, 2 * LANE - MLA_NOPE - MLA_ROPE))).reshape(QR, H * 2 * LANE)
    q = _proj(cq, wq, seq=S, rope="odd", tabs=tabs, scale=LOG2E * (MLA_NOPE + MLA_ROPE) ** -0.5)
    kv = _proj(ckv, w_kv_up.astype(BF16), seq=S)
    return _mla_attn(q, kv, kr, B=B, S=S)


def _gqa_mixer(h, hb, B, S, w_qkv, q_norm, k_norm):
    HD = q_norm.shape[0]
    Hk = GQA_KV_HEADS
    nq = w_qkv.shape[1] - 2 * Hk * HD
    half = HD // 2
    pos = jnp.arange(S)
    tabs = _rope_tables([_rope_cos_sin(pos // GRID_W, half), _rope_cos_sin(pos % GRID_W, half)], S)
    w = w_qkv.astype(BF16)
    q = _proj(hb, w[:, :nq], seq=S, norm="group", gain=q_norm, rope="all", tabs=tabs, scale=LOG2E * HD ** -0.5)
    k = _proj(hb, w[:, nq:nq + Hk * HD], seq=S, norm="group", gain=k_norm, rope="all", tabs=tabs)
    v = _proj(hb, w[:, nq + Hk * HD:], seq=S)
    return _gqa_attn(q, k, v, B=B, S=S)


def _na_mixer(h, hb, B, S, w_qkv, rpb):
    D = w_qkv.shape[1] // 3
    w = w_qkv.astype(BF16)
    q = _proj(hb, w[:, :D], seq=S, scale=LOG2E * LANE ** -0.5)
    kv = _proj(hb, w[:, D:], seq=S)
    return _na_attn(q, kv, _na_bias(rpb, S=S), B=B, S=S)


def _diff_mixer(h, hb, B, S, w_qkv, lam_vecs, subln, lam_init):
    D = w_qkv.shape[1] // 3
    w = w_qkv.astype(BF16)
    q = _proj(hb, w[:, :D], seq=S, scale=LOG2E * DIFF_HD ** -0.5)
    kv = _proj(hb, w[:, D:], seq=S)
    return _diff_attn(q, kv, lam_vecs, subln, B=B, S=S, lam_init=lam_init)


def kernel(x, mla_w_in, mla_q_norm, mla_w_q_up, mla_kv_norm, mla_w_kv_up, mla_w_o, gqa_w_qkv, gqa_q_norm, gqa_k_norm, gqa_w_o, na_w_qkv, na_rpb, na_w_o, diff_w_qkv, diff_lambda_q1, diff_lambda_k1, diff_lambda_q2, diff_lambda_k2, diff_subln, diff_w_o, ffn_w_gate, ffn_w_up, ffn_w_down, ln_mix_g, ln_mix_b, ln_ffn_g, ln_ffn_b):
    B, S, D = x.shape
    depth = ffn_w_gate.shape[0]
    alpha = (2.0 * depth) ** 0.25
    h = x.reshape(B * S, D)
    hb = h
    wg_all, wu_all, wd_all = (w.astype(BF16) for w in (ffn_w_gate, ffn_w_up, ffn_w_down))
    for i in range(depth):
        m = i % 4
        if m == 0:
            o = _mla_mixer(h, hb, B, S, mla_w_in, mla_q_norm, mla_w_q_up, mla_kv_norm, mla_w_kv_up)
            w_o = mla_w_o
        elif m == 1:
            o = _gqa_mixer(h, hb, B, S, gqa_w_qkv, gqa_q_norm, gqa_k_norm)
            w_o = gqa_w_o
        elif m == 2:
            o = _na_mixer(h, hb, B, S, na_w_qkv, na_rpb)
            w_o = na_w_o
        else:
            lam_init = 0.8 - 0.6 * math.exp(-0.3 * i)
            o = _diff_mixer(h, hb, B, S, diff_w_qkv,
                            (diff_lambda_q1, diff_lambda_k1, diff_lambda_q2, diff_lambda_k2),
                            diff_subln, lam_init)
            w_o = diff_w_o
        h, hb = _mm_ln(o, w_o.astype(BF16), h, ln_mix_g[i], ln_mix_b[i], alpha=alpha)
        hid = _ffn_up(hb, wg_all, wu_all, i)
        h, hb = _mm_ln(hid, wd_all, h, ln_ffn_g[i], ln_ffn_b[i], alpha=alpha, layer=i)
    return h.reshape(B, S, D)
```

```python
import functools
import math

import jax
import jax.numpy as jnp
from jax import lax
from jax.experimental import pallas as pl
from jax.experimental.pallas import tpu as pltpu

F32 = jnp.float32
BF16 = jnp.bfloat16

LANE = 128
VMEM_LIMIT_BYTES = 56 * 1024 * 1024
VMEM_SLACK_BYTES = 8 * 1024 * 1024
ROT_HALF = 32

GRID_W = 64
Q_BLOCK = 128
ROPE_THETA = 10000.0
NEG_INF = -1e30
MLA_NOPE, MLA_ROPE, MLA_V = 128, 64, 128
GQA_KV_HEADS = 4
NA_KH_MAX, NA_KW = 8, 16
NA_WIN_BLOCKS = 5
NA_DC_PAD = 32
DIFF_HD = 128
LN_EPS = 1e-5
RMS_EPS = 1e-6
LOG2E = math.log2(math.e)
FLASH_BODY_KEYS = 16384
PROJ_ROW_BLOCK = 256
DIFF_BODY_KEYS = 8192


def _cparams(sem, **kw):
    return pltpu.CompilerParams(dimension_semantics=sem, vmem_limit_bytes=VMEM_LIMIT_BYTES, **kw)


def _rotate_pairs(a, c, s1, s2):
    x2_under_x1 = pltpu.roll(a, LANE - ROT_HALF, 1)
    x1_under_x2 = pltpu.roll(a, ROT_HALF, 1)
    return a * c + x2_under_x1 * s1 + x1_under_x2 * s2


def _proj_kernel(*refs, n_groups, norm, rope, scale):
    it = iter(refs)
    x_ref, w_ref = next(it), next(it)
    g_ref = next(it) if norm is not None else None
    tabs = (next(it), next(it), next(it)) if rope is not None else None
    o_ref = next(it)
    bm = x_ref.shape[0]
    rb = min(bm, PROJ_ROW_BLOCK)
    for r in range(bm // rb):
        rows = slice(r * rb, (r + 1) * rb)
        acc = jnp.dot(x_ref[rows, :].astype(BF16), w_ref[...], preferred_element_type=F32)
        if norm == "full":
            ms = jnp.mean(acc * acc, axis=-1, keepdims=True)
            acc = acc * lax.rsqrt(ms + RMS_EPS) * g_ref[...]
        for g in range(n_groups):
            a = acc[:, g * LANE:(g + 1) * LANE]
            if norm == "group":
                ms = jnp.mean(a * a, axis=-1, keepdims=True)
                a = a * lax.rsqrt(ms + RMS_EPS) * g_ref[...]
            if rope == "all" or (rope == "odd" and g % 2 == 1):
                a = _rotate_pairs(a, tabs[0][rows, :], tabs[1][rows, :], tabs[2][rows, :])
            if scale != 1.0:
                a = a * scale
            o_ref[rows, g * LANE:(g + 1) * LANE] = a.astype(o_ref.dtype)


def _proj(x, w, *, seq, norm=None, gain=None, rope=None, tabs=None, scale=1.0, bm=1024, bn=None):
    T, K = x.shape
    N = w.shape[1]
    if bn is None:
        cap = 1024 if K > 512 else 2048
        bn = N if N <= cap else cap
    bm = min(bm, seq)
    assert T % bm == 0 and seq % bm == 0 and N % bn == 0 and bn % LANE == 0
    if norm == "full":
        assert bn == N
    if rope == "odd":
        assert (bn // LANE) % 2 == 0
    in_specs = [pl.BlockSpec((bm, K), lambda i, j: (i, 0)),
                pl.BlockSpec((K, bn), lambda i, j: (0, j))]
    args = [x, w]
    if norm == "full":
        in_specs.append(pl.BlockSpec((1, bn), lambda i, j: (0, 0)))
        args.append(gain.reshape(1, N).astype(F32))
    elif norm == "group":
        in_specs.append(pl.BlockSpec((1, LANE), lambda i, j: (0, 0)))
        args.append(gain.reshape(1, LANE).astype(F32))
    if rope is not None:
        nseq = seq // bm
        for t in tabs:
            in_specs.append(pl.BlockSpec((bm, LANE), lambda i, j: (i % nseq, 0)))
            args.append(t)
    kern = functools.partial(_proj_kernel, n_groups=bn // LANE, norm=norm, rope=rope, scale=scale)
    return pl.pallas_call(
        kern,
        grid=(T // bm, N // bn),
        in_specs=in_specs,
        out_specs=pl.BlockSpec((bm, bn), lambda i, j: (i, j)),
        out_shape=jax.ShapeDtypeStruct((T, N), BF16),
        compiler_params=_cparams(("parallel", "parallel")),
        name="proj",
    )(*args)


def _mla_down_kernel(x_ref, w_ref, gq_ref, gkv_ref, c_ref, s1_ref, s2_ref, cq_ref, ckv_ref, kr_ref):
    bm = x_ref.shape[0]
    rb = min(bm, PROJ_ROW_BLOCK)
    nq, nkv = cq_ref.shape[1], ckv_ref.shape[1]
    for r in range(bm // rb):
        rows = slice(r * rb, (r + 1) * rb)
        acc = jnp.dot(x_ref[rows, :].astype(BF16), w_ref[...], preferred_element_type=F32)
        for lo, n, g_ref, o_ref in ((0, nq, gq_ref, cq_ref), (nq, nkv, gkv_ref, ckv_ref)):
            a = acc[:, lo:lo + n]
            ms = jnp.mean(a * a, axis=-1, keepdims=True)
            o_ref[rows, :] = (a * lax.rsqrt(ms + RMS_EPS) * g_ref[...]).astype(o_ref.dtype)
        a = _rotate_pairs(acc[:, nq + nkv:], c_ref[rows, :], s1_ref[rows, :], s2_ref[rows, :])
        kr_ref[rows, :] = a.astype(kr_ref.dtype)


def _mla_down(x, w, gq, gkv, tabs, *, seq, bm=1024):
    T, D = x.shape
    nq, nkv = gq.shape[0], gkv.shape[0]
    N = w.shape[1]
    assert N == nq + nkv + LANE and nq % LANE == 0 and nkv % LANE == 0
    bm = min(bm, seq)
    assert T % bm == 0 and seq % bm == 0
    nseq = seq // bm
    tab_spec = pl.BlockSpec((bm, LANE), lambda i: (i % nseq, 0))
    return pl.pallas_call(
        _mla_down_kernel,
        grid=(T // bm,),
        in_specs=[pl.BlockSpec((bm, D), lambda i: (i, 0)),
                  pl.BlockSpec((D, N), lambda i: (0, 0)),
                  pl.BlockSpec((1, nq), lambda i: (0, 0)),
                  pl.BlockSpec((1, nkv), lambda i: (0, 0)),
                  tab_spec, tab_spec, tab_spec],
        out_specs=[pl.BlockSpec((bm, nq), lambda i: (i, 0)),
                   pl.BlockSpec((bm, nkv), lambda i: (i, 0)),
                   pl.BlockSpec((bm, LANE), lambda i: (i, 0))],
        out_shape=[jax.ShapeDtypeStruct((T, nq), BF16), jax.ShapeDtypeStruct((T, nkv), BF16),
                   jax.ShapeDtypeStruct((T, LANE), BF16)],
        compiler_params=_cparams(("parallel",)),
        name="mla_down",
    )(x, w, gq.reshape(1, nq).astype(F32), gkv.reshape(1, nkv).astype(F32), *tabs)


def _mm_ln_kernel(a_ref, w_ref, h_ref, g_ref, b_ref, of_ref, ob_ref, *, alpha):
    y = alpha * h_ref[...] + jnp.dot(a_ref[...], w_ref[...], preferred_element_type=F32)
    mu = jnp.mean(y, axis=-1, keepdims=True)
    d = y - mu
    var = jnp.mean(d * d, axis=-1, keepdims=True)
    out = d * lax.rsqrt(var + LN_EPS) * g_ref[...] + b_ref[...]
    of_ref[...] = out
    ob_ref[...] = out.astype(BF16)


def _mm_ln(a, w, h, g, b, *, alpha, layer=None):
    T, K = a.shape
    D = w.shape[-1]
    if layer is None:
        w_spec = pl.BlockSpec((K, D), lambda i: (0, 0), pipeline_mode=pl.Buffered(1))
    else:
        w_spec = pl.BlockSpec((None, K, D), lambda i: (layer, 0, 0), pipeline_mode=pl.Buffered(1))
    row_bytes = 2 * (2 * K) + 2 * (4 * D) + 2 * (4 * D) + 2 * (2 * D) + 4 * D
    budget = VMEM_LIMIT_BYTES - 2 * K * D - VMEM_SLACK_BYTES
    bm = 512
    while bm > 8 and (bm * row_bytes > budget or T % bm):
        bm //= 2
    assert T % bm == 0
    return pl.pallas_call(
        functools.partial(_mm_ln_kernel, alpha=alpha),
        grid=(T // bm,),
        in_specs=[pl.BlockSpec((bm, K), lambda i: (i, 0)),
                  w_spec,
                  pl.BlockSpec((bm, D), lambda i: (i, 0)),
                  pl.BlockSpec((1, D), lambda i: (0, 0)),
                  pl.BlockSpec((1, D), lambda i: (0, 0))],
        out_specs=[pl.BlockSpec((bm, D), lambda i: (i, 0)),
                   pl.BlockSpec((bm, D), lambda i: (i, 0))],
        out_shape=[jax.ShapeDtypeStruct((T, D), F32), jax.ShapeDtypeStruct((T, D), BF16)],
        compiler_params=_cparams(("parallel",)),
        name="mm_ln",
    )(a, w, h, g.reshape(1, D).astype(F32), b.reshape(1, D).astype(F32))


def _ffn_up_kernel(x_ref, wg_ref, wu_ref, o_ref):
    x = x_ref[...]
    g = jnp.dot(x, wg_ref[...], preferred_element_type=F32)
    u = jnp.dot(x, wu_ref[...], preferred_element_type=F32)
    o_ref[...] = ((g / (1.0 + jnp.exp(-g))) * u).astype(o_ref.dtype)


def _ffn_up(x, wg, wu, layer, *, bm=1024, bn=512):
    T, D = x.shape
    Fh = wg.shape[-1]
    assert T % bm == 0 and Fh % bn == 0
    w_spec = pl.BlockSpec((None, D, bn), lambda i, j: (layer, 0, j))
    return pl.pallas_call(
        _ffn_up_kernel,
        grid=(T // bm, Fh // bn),
        in_specs=[pl.BlockSpec((bm, D), lambda i, j: (i, 0)), w_spec, w_spec],
        out_specs=pl.BlockSpec((bm, bn), lambda i, j: (i, j)),
        out_shape=jax.ShapeDtypeStruct((T, Fh), BF16),
        compiler_params=_cparams(("parallel", "parallel")),
        name="ffn_up",
    )(x, wg, wu)


def _chunk_start(c, tk):
    return c * tk if isinstance(c, int) else pl.multiple_of(c * tk, tk)


def _largest_divisor_leq(n, cap):
    return max(d for d in range(1, max(1, min(n, cap)) + 1) if n % d == 0)


def _flash_loop(score_fns, v_ref, s_refs, *, M, tk, body_keys=FLASH_BODY_KEYS):
    S, Dv = v_ref.shape
    n = S // tk
    assert n % 2 == 0
    npairs = n // 2

    def absorb(c, s_ref, mx_cst, carry):
        m, l, acc = carry
        mx, cst = mx_cst
        off = _chunk_start(c, tk)
        m_new = jnp.maximum(m, mx if cst is None else mx + cst)
        a = jnp.exp2(m - m_new)
        p = jnp.exp2(s_ref[...] - (m_new if cst is None else m_new - cst))
        l = a * l + jnp.sum(p, axis=0, keepdims=True)
        pv = lax.dot_general(v_ref[pl.ds(off, tk), :], p.astype(BF16), (((0,), (0,)), ((), ())),
                             preferred_element_type=F32)
        return m_new, l, a * acc + pv

    def pair(scores, issue_next, c0, mx0, state):
        mx1 = scores(c0 + 1, s_refs[1])
        state = absorb(c0, s_refs[0], mx0, state)
        mx0 = issue_next()
        state = absorb(c0 + 1, s_refs[1], mx1, state)
        return mx0, state

    init = (jnp.full((1, M), -jnp.inf, F32), jnp.zeros((1, M), F32), jnp.zeros((Dv, M), F32))
    out = []
    mx0 = score_fns[0](0, s_refs[0])
    for t, scores in enumerate(score_fns):
        state = init
        if npairs > 1:
            def body(jj, carry, scores=scores):
                return pair(scores, lambda: scores(2 * jj + 2, s_refs[0]), 2 * jj, *carry)

            unroll = _largest_divisor_leq(npairs - 1, max(1, body_keys // (2 * tk)))
            mx0, state = lax.fori_loop(0, npairs - 1, body, (mx0, state), unroll=unroll)
        if t + 1 < len(score_fns):
            issue_next = functools.partial(score_fns[t + 1], 0, s_refs[0])
        else:
            issue_next = lambda: None
        mx0, state = pair(scores, issue_next, 2 * (npairs - 1), mx0, state)
        out.append(state[1:])
    return out


def _flash_body(q_tiles, k_ref, v_ref, s_refs, *, tk):
    def make_scores(q):
        qt = q.astype(F32).T.astype(BF16)

        def scores(c, s_ref):
            st = jnp.dot(k_ref[pl.ds(_chunk_start(c, tk), tk), :], qt, preferred_element_type=F32)
            s_ref[...] = st
            return jnp.max(st, axis=0, keepdims=True), None

        return scores

    res = _flash_loop([make_scores(q) for q in q_tiles], v_ref, s_refs, M=q_tiles[0].shape[0], tk=tk)
    return [(acc * (1.0 / l)).T for l, acc in res]


def _gqa_attn_kernel(q_ref, k_ref, v_ref, o_ref, s0_ref, s1_ref, *, G, tq, tk):
    tiles = [slice(r, r + tq) for r in range(0, q_ref.shape[0], tq)]
    qs = [jnp.concatenate([q_ref[rows, g * LANE:(g + 1) * LANE] for g in range(G)], axis=0) for rows in tiles]
    for rows, o in zip(tiles, _flash_body(qs, k_ref, v_ref, (s0_ref, s1_ref), tk=tk)):
        for g in range(G):
            o_ref[rows, g * LANE:(g + 1) * LANE] = o[g * tq:(g + 1) * tq].astype(o_ref.dtype)


def _gqa_attn(q, k, v, *, B, S, tq=128, tk=1024, nt=4):
    H, Hk = q.shape[1] // LANE, k.shape[1] // LANE
    G = H // Hk
    tk = min(tk, S // 2)
    bq = nt * tq
    assert S % bq == 0 and S % tk == 0
    q3, k3, v3 = (t.reshape(B, S, t.shape[1]) for t in (q, k, v))
    kern = functools.partial(_gqa_attn_kernel, G=G, tq=tq, tk=tk)
    out = pl.pallas_call(
        kern,
        grid=(B, Hk, S // bq),
        in_specs=[pl.BlockSpec((None, bq, G * LANE), lambda b, h, i: (b, i, h)),
                  pl.BlockSpec((None, S, LANE), lambda b, h, i: (b, 0, h)),
                  pl.BlockSpec((None, S, LANE), lambda b, h, i: (b, 0, h))],
        out_specs=pl.BlockSpec((None, bq, G * LANE), lambda b, h, i: (b, i, h)),
        out_shape=jax.ShapeDtypeStruct((B, S, H * LANE), BF16),
        scratch_shapes=[pltpu.VMEM((tk, G * tq), F32)] * 2,
        compiler_params=_cparams(("parallel", "parallel", "parallel")),
        name="gqa_attn",
    )(q3, k3, v3)
    return out.reshape(B * S, H * LANE)


def _mla_attn_kernel(q_ref, kn_ref, kr_ref, v_ref, o_ref, kc_ref, s0_ref, s1_ref, *, tq, tk):
    @pl.when(pl.program_id(2) == 0)
    def _():
        kc_ref[:, :LANE] = kn_ref[...]
        kc_ref[:, LANE:] = kr_ref[...]

    tiles = [slice(r, r + tq) for r in range(0, q_ref.shape[0], tq)]
    outs = _flash_body([q_ref[rows, :] for rows in tiles], kc_ref, v_ref, (s0_ref, s1_ref), tk=tk)
    for rows, o in zip(tiles, outs):
        o_ref[rows, :] = o.astype(o_ref.dtype)


def _mla_attn(q, kv, kr, *, B, S, tq=512, tk=1024, nt=4):
    H = q.shape[1] // (2 * LANE)
    tq, tk = min(tq, S // nt), min(tk, S // 2)
    bq = nt * tq
    assert S % bq == 0 and S % tk == 0
    q3, kv3, kr3 = q.reshape(B, S, -1), kv.reshape(B, S, -1), kr.reshape(B, S, LANE)
    kern = functools.partial(_mla_attn_kernel, tq=tq, tk=tk)
    out = pl.pallas_call(
        kern,
        grid=(B, H, S // bq),
        in_specs=[pl.BlockSpec((None, bq, 2 * LANE), lambda b, h, i: (b, i, h)),
                  pl.BlockSpec((None, S, LANE), lambda b, h, i: (b, 0, 2 * h), pipeline_mode=pl.Buffered(1)),
                  pl.BlockSpec((None, S, LANE), lambda b, h, i: (b, 0, 0), pipeline_mode=pl.Buffered(1)),
                  pl.BlockSpec((None, S, LANE), lambda b, h, i: (b, 0, 2 * h + 1))],
        out_specs=pl.BlockSpec((None, bq, LANE), lambda b, h, i: (b, i, h)),
        out_shape=jax.ShapeDtypeStruct((B, S, H * LANE), BF16),
        scratch_shapes=[pltpu.VMEM((S, 2 * LANE), BF16)] + [pltpu.VMEM((tk, tq), F32)] * 2,
        compiler_params=_cparams(("arbitrary", "arbitrary", "arbitrary")),
        name="mla_attn",
    )(q3, kv3, kr3, kv3)
    return out.reshape(B * S, H * LANE)


def _diff_attn_kernel(slopes_ref, q_ref, k_ref, v_ref, lq1_ref, lk1_ref, lq2_ref, lk2_ref, sub_ref, o_ref,
                      s0_ref, s1_ref, t_ref, *, tq, tk, lam_init):
    slope = slopes_ref[pl.program_id(1)]
    arel = slope * (lax.broadcasted_iota(jnp.int32, (tk, tq), 0)
                    - lax.broadcasted_iota(jnp.int32, (tk, tq), 1)).astype(F32)
    t_ref[0] = arel
    t_ref[1] = -arel
    tiles = [slice(r, r + tq) for r in range(0, q_ref.shape[0], tq)]

    def make_scores(t, rows):
        q0 = pl.program_id(2) * q_ref.shape[0] + t * tq
        c_diag = q0 // tk
        t_ref[2 + t] = -jnp.abs(arel + slope * (c_diag * tk - q0).astype(F32))
        qts = [q_ref[rows, c * DIFF_HD:(c + 1) * DIFF_HD].astype(F32).T.astype(BF16) for c in range(2)]

        def scores(c, s_ref):
            off = _chunk_start(c, tk)
            e = slope * (off - q0).astype(F32)
            side = jnp.where(c == c_diag, 2 + t, jnp.where(c > c_diag, 1, 0))
            cst = jnp.where(c == c_diag, 0.0, jnp.where(c > c_diag, -e, e))
            bias = t_ref[side]
            mxs = []
            for comp in range(2):
                ks = k_ref[pl.ds(off, tk), comp * DIFF_HD:(comp + 1) * DIFF_HD]
                st = jnp.dot(ks, qts[comp], preferred_element_type=F32) + bias
                s_ref[:, comp * tq:(comp + 1) * tq] = st
                mxs.append(jnp.max(st, axis=0, keepdims=True))
            return jnp.concatenate(mxs, axis=1), cst

        return scores

    res = _flash_loop([make_scores(t, rows) for t, rows in enumerate(tiles)], v_ref, (s0_ref, s1_ref),
                      M=2 * tq, tk=tk, body_keys=DIFF_BODY_KEYS)
    lam = (jnp.exp(jnp.sum(lq1_ref[...] * lk1_ref[...], axis=-1, keepdims=True))
           - jnp.exp(jnp.sum(lq2_ref[...] * lk2_ref[...], axis=-1, keepdims=True)) + lam_init)
    for rows, (l, acc) in zip(tiles, res):
        on = acc * (1.0 / l)
        o = (on[:, :tq] - lam * on[:, tq:]).T
        ms = jnp.mean(o * o, axis=-1, keepdims=True)
        o = o * lax.rsqrt(ms + RMS_EPS) * sub_ref[...] * (1.0 - lam_init)
        o_ref[rows, :] = o.astype(o_ref.dtype)


def _diff_attn(q, kv, lam_vecs, subln, *, B, S, lam_init, tq=256, tk=512, nt=4):
    W = 2 * DIFF_HD
    H = q.shape[1] // W
    tq, tk = min(tq, S // nt), min(tk, S // 2)
    bq = nt * tq
    assert S % bq == 0 and S % tk == 0 and tk % tq == 0
    slopes = LOG2E * jnp.exp2(-8.0 * jnp.arange(1, H + 1, dtype=F32) / H)
    q3, kv3 = q.reshape(B, S, -1), kv.reshape(B, S, -1)
    vec = pl.BlockSpec((1, DIFF_HD), lambda b, h, i: (0, 0))
    kern = functools.partial(_diff_attn_kernel, tq=tq, tk=tk, lam_init=lam_init)
    out = pl.pallas_call(
        kern,
        grid=(B, H, S // bq),
        in_specs=[pl.BlockSpec(memory_space=pltpu.SMEM),
                  pl.BlockSpec((None, bq, W), lambda b, h, i: (b, i, h)),
                  pl.BlockSpec((None, S, W), lambda b, h, i: (b, 0, h)),
                  pl.BlockSpec((None, S, W), lambda b, h, i: (b, 0, H + h)),
                  vec, vec, vec, vec,
                  pl.BlockSpec((1, W), lambda b, h, i: (0, 0))],
        out_specs=pl.BlockSpec((None, bq, W), lambda b, h, i: (b, i, h)),
        out_shape=jax.ShapeDtypeStruct((B, S, H * W), BF16),
        scratch_shapes=[pltpu.VMEM((tk, 2 * tq), F32)] * 2
                       + [pltpu.VMEM((2 + nt, tk, tq), F32)],
        compiler_params=_cparams(("parallel", "parallel", "parallel")),
        name="diff_attn",
    )(slopes, q3, kv3, kv3, *[v.reshape(1, DIFF_HD).astype(F32) for v in lam_vecs],
      subln.reshape(1, W).astype(F32))
    return out.reshape(B * S, H * W)


def _na_rep_block(p, nblk):
    return jnp.where(p < 3, p, nblk - NA_WIN_BLOCKS + p)


def _na_bias_kernel(rpbT_ref, o_ref, *, nblk, R):
    nk = NA_WIN_BLOCKS * Q_BLOCK
    rows_per_blk = Q_BLOCK // GRID_W
    kh = min(NA_KH_MAX, R)
    j = _na_rep_block(pl.program_id(0), nblk)
    ws = jnp.clip(j - 2, 0, nblk - NA_WIN_BLOCKS)
    tab = rpbT_ref[...]
    n_dr = 2 * NA_KH_MAX - 1
    n_dc = 2 * NA_KW - 1

    w_shift = GRID_W.bit_length() - 1
    n1 = lax.broadcasted_iota(jnp.int32, (NA_DC_PAD, nk), 1)
    krow1 = rows_per_blk * ws + (n1 >> w_shift)
    halves = []
    for a in range(rows_per_blk):
        qrow = rows_per_blk * j + a
        drow1 = jnp.clip(krow1 - qrow + NA_KH_MAX - 1, 0, n_dr - 1)
        sel = jnp.zeros((NA_DC_PAD, nk), F32)
        for dr in range(n_dr):
            sel = jnp.where(drow1 == dr, tab[:, dr:dr + 1], sel)
        qc = lax.broadcasted_iota(jnp.int32, (GRID_W, nk), 0)
        n = lax.broadcasted_iota(jnp.int32, (GRID_W, nk), 1)
        kc = n & (GRID_W - 1)
        krow = rows_per_blk * ws + (n >> w_shift)
        rs = jnp.clip(qrow - kh // 2, 0, R - kh)
        cs = jnp.clip(qc - NA_KW // 2, 0, GRID_W - NA_KW)
        ok = (kc >= cs) & (kc < cs + NA_KW) & (krow >= rs) & (krow < rs + kh)
        dcol = jnp.clip(kc - qc + NA_KW - 1, 0, n_dc - 1)
        bias = jnp.zeros((GRID_W, nk), F32)
        for dc in range(n_dc):
            bias = jnp.where(dcol == dc, sel[dc:dc + 1, :], bias)
        halves.append(jnp.where(ok, bias * LOG2E, NEG_INF))
    o_ref[...] = jnp.concatenate(halves, axis=0).T


def _na_bias(rpb, *, S):
    H = rpb.shape[0]
    nblk, R = S // Q_BLOCK, S // GRID_W
    nk = NA_WIN_BLOCKS * Q_BLOCK
    rpbT = jnp.transpose(rpb.astype(F32), (0, 2, 1))
    rpbT = jnp.pad(rpbT, ((0, 0), (0, NA_DC_PAD - rpbT.shape[1]), (0, LANE - rpbT.shape[2])))
    kern = functools.partial(_na_bias_kernel, nblk=nblk, R=R)
    return pl.pallas_call(
        kern,
        grid=(NA_WIN_BLOCKS, H),
        in_specs=[pl.BlockSpec((None, NA_DC_PAD, LANE), lambda p, h: (h, 0, 0))],
        out_specs=pl.BlockSpec((None, None, nk, Q_BLOCK), lambda p, h: (p, h, 0, 0)),
        out_shape=jax.ShapeDtypeStruct((NA_WIN_BLOCKS, H, nk, Q_BLOCK), F32),
        compiler_params=_cparams(("parallel", "parallel")),
        name="na_bias",
    )(rpbT)


def _na_attn_kernel(q_ref, *refs, H):
    k_refs = refs[:NA_WIN_BLOCKS]
    v_refs = refs[NA_WIN_BLOCKS:2 * NA_WIN_BLOCKS]
    bias_ref, o_ref = refs[2 * NA_WIN_BLOCKS], refs[2 * NA_WIN_BLOCKS + 1]
    for h in range(H):
        hs = slice(h * LANE, (h + 1) * LANE)
        k = jnp.concatenate([r[:, hs] for r in k_refs], axis=0)
        v = jnp.concatenate([r[:, hs] for r in v_refs], axis=0)
        st = lax.dot_general(k, q_ref[:, hs], (((1,), (1,)), ((), ())), preferred_element_type=F32)
        st = st + bias_ref[h]
        m = jnp.max(st, axis=0, keepdims=True)
        p = jnp.exp2(st - m)
        l = jnp.sum(p, axis=0, keepdims=True)
        ot = lax.dot_general(v, p.astype(BF16), (((0,), (0,)), ((), ())), preferred_element_type=F32)
        o_ref[:, hs] = (ot * (1.0 / l)).T.astype(o_ref.dtype)


def _na_attn(q, kv, bias, *, B, S):
    D = q.shape[1]
    H = D // LANE
    nblk = S // Q_BLOCK
    assert S % Q_BLOCK == 0 and nblk >= NA_WIN_BLOCKS
    q3, kv3 = q.reshape(B, S, D), kv.reshape(B, S, 2 * D)

    def win(j):
        return jnp.clip(j - 2, 0, nblk - NA_WIN_BLOCKS)

    def pattern(j):
        return jnp.where(j < 2, j, jnp.where(j <= nblk - 3, 2, j - (nblk - NA_WIN_BLOCKS)))

    k_specs = [pl.BlockSpec((None, Q_BLOCK, D), lambda b, j, t=t: (b, win(j) + t, 0))
               for t in range(NA_WIN_BLOCKS)]
    v_specs = [pl.BlockSpec((None, Q_BLOCK, D), lambda b, j, t=t: (b, win(j) + t, 1))
               for t in range(NA_WIN_BLOCKS)]
    nk = NA_WIN_BLOCKS * Q_BLOCK
    out = pl.pallas_call(
        functools.partial(_na_attn_kernel, H=H),
        grid=(B, nblk),
        in_specs=[pl.BlockSpec((None, Q_BLOCK, D), lambda b, j: (b, j, 0))] + k_specs + v_specs
                 + [pl.BlockSpec((None, H, nk, Q_BLOCK), lambda b, j: (pattern(j), 0, 0, 0))],
        out_specs=pl.BlockSpec((None, Q_BLOCK, D), lambda b, j: (b, j, 0)),
        out_shape=jax.ShapeDtypeStruct((B, S, D), BF16),
        compiler_params=_cparams(("parallel", "parallel")),
        name="na_attn",
    )(q3, *([kv3] * (2 * NA_WIN_BLOCKS)), bias)
    return out.reshape(B * S, D)


def _rope_cos_sin(pos, dim):
    inv = 1.0 / (ROPE_THETA ** (jnp.arange(0, dim, 2, dtype=F32) / dim))
    ang = pos.astype(F32)[:, None] * inv[None, :]
    return jnp.cos(ang), jnp.sin(ang)


def _rope_tables(pairs, S):
    z = jnp.zeros((S, ROT_HALF), F32)
    c_parts, s1_parts, s2_parts = [], [], []
    for cos, sin in pairs:
        c_parts += [cos, cos]
        s1_parts += [-sin, z]
        s2_parts += [z, sin]
    pad = LANE - 2 * ROT_HALF * len(pairs)
    if pad:
        zp = jnp.zeros((S, pad), F32)
        c_parts.append(zp), s1_parts.append(zp), s2_parts.append(zp)
    return tuple(jnp.concatenate(p, axis=1) for p in (c_parts, s1_parts, s2_parts))


def _mla_mixer(h, hb, B, S, w_in, q_norm, w_q_up, kv_norm, w_kv_up):
    QR, KVR = q_norm.shape[0], kv_norm.shape[0]
    H = w_q_up.shape[1] // (MLA_NOPE + MLA_ROPE)
    tabs = _rope_tables([_rope_cos_sin(jnp.arange(S), MLA_ROPE)], S)
    w_in = jnp.pad(w_in.astype(BF16), ((0, 0), (0, LANE - MLA_ROPE)))
    cq, ckv, kr = _mla_down(hb, w_in, q_norm, kv_norm, tabs, seq=S)
    wq = w_q_up.astype(BF16).reshape(QR, H, MLA_NOPE + MLA_ROPE)
    wq = jnp.pad(wq, ((0, 0), (0, 0), (0, 2 * LANE - MLA_NOPE - MLA_ROPE))).reshape(QR, H * 2 * LANE)
    q = _proj(cq, wq, seq=S, rope="odd", tabs=tabs, scale=LOG2E * (MLA_NOPE + MLA_ROPE) ** -0.5)
    kv = _proj(ckv, w_kv_up.astype(BF16), seq=S)
    return _mla_attn(q, kv, kr, B=B, S=S)


def _gqa_mixer(h, hb, B, S, w_qkv, q_norm, k_norm):
    HD = q_norm.shape[0]
    Hk = GQA_KV_HEADS
    nq = w_qkv.shape[1] - 2 * Hk * HD
    half = HD // 2
    pos = jnp.arange(S)
    tabs = _rope_tables([_rope_cos_sin(pos // GRID_W, half), _rope_cos_sin(pos % GRID_W, half)], S)
    w = w_qkv.astype(BF16)
    q = _proj(hb, w[:, :nq], seq=S, norm="group", gain=q_norm, rope="all", tabs=tabs, scale=LOG2E * HD ** -0.5)
    k = _proj(hb, w[:, nq:nq + Hk * HD], seq=S, norm="group", gain=k_norm, rope="all", tabs=tabs)
    v = _proj(hb, w[:, nq + Hk * HD:], seq=S)
    return _gqa_attn(q, k, v, B=B, S=S)


def _na_mixer(h, hb, B, S, w_qkv, rpb):
    D = w_qkv.shape[1] // 3
    w = w_qkv.astype(BF16)
    q = _proj(hb, w[:, :D], seq=S, scale=LOG2E * LANE ** -0.5)
    kv = _proj(hb, w[:, D:], seq=S)
    return _na_attn(q, kv, _na_bias(rpb, S=S), B=B, S=S)


def _diff_mixer(h, hb, B, S, w_qkv, lam_vecs, subln, lam_init):
    D = w_qkv.shape[1] // 3
    w = w_qkv.astype(BF16)
    q = _proj(hb, w[:, :D], seq=S, scale=LOG2E * DIFF_HD ** -0.5)
    kv = _proj(hb, w[:, D:], seq=S)
    return _diff_attn(q, kv, lam_vecs, subln, B=B, S=S, lam_init=lam_init)


def kernel(x, mla_w_in, mla_q_norm, mla_w_q_up, mla_kv_norm, mla_w_kv_up, mla_w_o, gqa_w_qkv, gqa_q_norm, gqa_k_norm, gqa_w_o, na_w_qkv, na_rpb, na_w_o, diff_w_qkv, diff_lambda_q1, diff_lambda_k1, diff_lambda_q2, diff_lambda_k2, diff_subln, diff_w_o, ffn_w_gate, ffn_w_up, ffn_w_down, ln_mix_g, ln_mix_b, ln_ffn_g, ln_ffn_b):
    B, S, D = x.shape
    depth = ffn_w_gate.shape[0]
    alpha = (2.0 * depth) ** 0.25
    h = x.reshape(B * S, D)
    hb = h
    wg_all, wu_all, wd_all = (w.astype(BF16) for w in (ffn_w_gate, ffn_w_up, ffn_w_down))
    for i in range(depth):
        m = i % 4
        if m == 0:
            o = _mla_mixer(h, hb, B, S, mla_w_in, mla_q_norm, mla_w_q_up, mla_kv_norm, mla_w_kv_up)
            w_o = mla_w_o
        elif m == 1:
            o = _gqa_mixer(h, hb, B, S, gqa_w_qkv, gqa_q_norm, gqa_k_norm)
            w_o = gqa_w_o
        elif m == 2:
            o = _na_mixer(h, hb, B, S, na_w_qkv, na_rpb)
            w_o = na_w_o
        else:
            lam_init = 0.8 - 0.6 * math.exp(-0.3 * i)
            o = _diff_mixer(h, hb, B, S, diff_w_qkv,
                            (diff_lambda_q1, diff_lambda_k1, diff_lambda_q2, diff_lambda_k2),
                            diff_subln, lam_init)
            w_o = diff_w_o
        h, hb = _mm_ln(o, w_o.astype(BF16), h, ln_mix_g[i], ln_mix_b[i], alpha=alpha)
        hid = _ffn_up(hb, wg_all, wu_all, i)
        h, hb = _mm_ln(hid, wd_all, h, ln_ffn_g[i], ln_ffn_b[i], alpha=alpha, layer=i)
    return h.reshape(B, S, D)
```

```python
import functools
import math

import jax
import jax.numpy as jnp
from jax import lax
from jax.experimental import pallas as pl
from jax.experimental.pallas import tpu as pltpu

F32 = jnp.float32
BF16 = jnp.bfloat16

LANE = 128
VMEM_LIMIT_BYTES = 56 * 1024 * 1024
VMEM_SLACK_BYTES = 8 * 1024 * 1024
ROT_HALF = 32

GRID_W = 64
Q_BLOCK = 128
ROPE_THETA = 10000.0
NEG_INF = -1e30
MLA_NOPE, MLA_ROPE, MLA_V = 128, 64, 128
GQA_KV_HEADS = 4
NA_KH_MAX, NA_KW = 8, 16
NA_WIN_BLOCKS = 5
NA_DC_PAD = 32
DIFF_HD = 128
LN_EPS = 1e-5
RMS_EPS = 1e-6
LOG2E = math.log2(math.e)
FLASH_BODY_KEYS = 16384
ABSORB_SUB_KEYS = 256
PROJ_ROW_BLOCK = 256
DIFF_BODY_KEYS = 8192


def _cparams(sem, **kw):
    return pltpu.CompilerParams(dimension_semantics=sem, vmem_limit_bytes=VMEM_LIMIT_BYTES, **kw)


def _rotate_pairs(a, c, s1, s2):
    x2_under_x1 = pltpu.roll(a, LANE - ROT_HALF, 1)
    x1_under_x2 = pltpu.roll(a, ROT_HALF, 1)
    return a * c + x2_under_x1 * s1 + x1_under_x2 * s2


def _proj_kernel(*refs, n_groups, norm, rope, scale):
    it = iter(refs)
    x_ref, w_ref = next(it), next(it)
    g_ref = next(it) if norm is not None else None
    tabs = (next(it), next(it), next(it)) if rope is not None else None
    o_ref = next(it)
    bm = x_ref.shape[0]
    rb = min(bm, PROJ_ROW_BLOCK)
    for r in range(bm // rb):
        rows = slice(r * rb, (r + 1) * rb)
        acc = jnp.dot(x_ref[rows, :].astype(BF16), w_ref[...], preferred_element_type=F32)
        if norm == "full":
            ms = jnp.mean(acc * acc, axis=-1, keepdims=True)
            acc = acc * lax.rsqrt(ms + RMS_EPS) * g_ref[...]
        for g in range(n_groups):
            a = acc[:, g * LANE:(g + 1) * LANE]
            if norm == "group":
                ms = jnp.mean(a * a, axis=-1, keepdims=True)
                a = a * lax.rsqrt(ms + RMS_EPS) * g_ref[...]
            if rope == "all" or (rope == "odd" and g % 2 == 1):
                a = _rotate_pairs(a, tabs[0][rows, :], tabs[1][rows, :], tabs[2][rows, :])
            if scale != 1.0:
                a = a * scale
            o_ref[rows, g * LANE:(g + 1) * LANE] = a.astype(o_ref.dtype)


def _proj(x, w, *, seq, norm=None, gain=None, rope=None, tabs=None, scale=1.0, bm=1024, bn=None):
    T, K = x.shape
    N = w.shape[1]
    if bn is None:
        cap = 1024 if K > 512 else 2048
        bn = N if N <= cap else cap
    bm = min(bm, seq)
    assert T % bm == 0 and seq % bm == 0 and N % bn == 0 and bn % LANE == 0
    if norm == "full":
        assert bn == N
    if rope == "odd":
        assert (bn // LANE) % 2 == 0
    in_specs = [pl.BlockSpec((bm, K), lambda i, j: (i, 0)),
                pl.BlockSpec((K, bn), lambda i, j: (0, j))]
    args = [x, w]
    if norm == "full":
        in_specs.append(pl.BlockSpec((1, bn), lambda i, j: (0, 0)))
        args.append(gain.reshape(1, N).astype(F32))
    elif norm == "group":
        in_specs.append(pl.BlockSpec((1, LANE), lambda i, j: (0, 0)))
        args.append(gain.reshape(1, LANE).astype(F32))
    if rope is not None:
        nseq = seq // bm
        for t in tabs:
            in_specs.append(pl.BlockSpec((bm, LANE), lambda i, j: (i % nseq, 0)))
            args.append(t)
    kern = functools.partial(_proj_kernel, n_groups=bn // LANE, norm=norm, rope=rope, scale=scale)
    return pl.pallas_call(
        kern,
        grid=(T // bm, N // bn),
        in_specs=in_specs,
        out_specs=pl.BlockSpec((bm, bn), lambda i, j: (i, j)),
        out_shape=jax.ShapeDtypeStruct((T, N), BF16),
        compiler_params=_cparams(("parallel", "parallel")),
        name="proj",
    )(*args)


def _mla_down_kernel(x_ref, w_ref, gq_ref, gkv_ref, c_ref, s1_ref, s2_ref, cq_ref, ckv_ref, kr_ref):
    bm = x_ref.shape[0]
    rb = min(bm, PROJ_ROW_BLOCK)
    nq, nkv = cq_ref.shape[1], ckv_ref.shape[1]
    for r in range(bm // rb):
        rows = slice(r * rb, (r + 1) * rb)
        acc = jnp.dot(x_ref[rows, :].astype(BF16), w_ref[...], preferred_element_type=F32)
        for lo, n, g_ref, o_ref in ((0, nq, gq_ref, cq_ref), (nq, nkv, gkv_ref, ckv_ref)):
            a = acc[:, lo:lo + n]
            ms = jnp.mean(a * a, axis=-1, keepdims=True)
            o_ref[rows, :] = (a * lax.rsqrt(ms + RMS_EPS) * g_ref[...]).astype(o_ref.dtype)
        a = _rotate_pairs(acc[:, nq + nkv:], c_ref[rows, :], s1_ref[rows, :], s2_ref[rows, :])
        kr_ref[rows, :] = a.astype(kr_ref.dtype)


def _mla_down(x, w, gq, gkv, tabs, *, seq, bm=1024):
    T, D = x.shape
    nq, nkv = gq.shape[0], gkv.shape[0]
    N = w.shape[1]
    assert N == nq + nkv + LANE and nq % LANE == 0 and nkv % LANE == 0
    bm = min(bm, seq)
    assert T % bm == 0 and seq % bm == 0
    nseq = seq // bm
    tab_spec = pl.BlockSpec((bm, LANE), lambda i: (i % nseq, 0))
    return pl.pallas_call(
        _mla_down_kernel,
        grid=(T // bm,),
        in_specs=[pl.BlockSpec((bm, D), lambda i: (i, 0)),
                  pl.BlockSpec((D, N), lambda i: (0, 0)),
                  pl.BlockSpec((1, nq), lambda i: (0, 0)),
                  pl.BlockSpec((1, nkv), lambda i: (0, 0)),
                  tab_spec, tab_spec, tab_spec],
        out_specs=[pl.BlockSpec((bm, nq), lambda i: (i, 0)),
                   pl.BlockSpec((bm, nkv), lambda i: (i, 0)),
                   pl.BlockSpec((bm, LANE), lambda i: (i, 0))],
        out_shape=[jax.ShapeDtypeStruct((T, nq), BF16), jax.ShapeDtypeStruct((T, nkv), BF16),
                   jax.ShapeDtypeStruct((T, LANE), BF16)],
        compiler_params=_cparams(("parallel",)),
        name="mla_down",
    )(x, w, gq.reshape(1, nq).astype(F32), gkv.reshape(1, nkv).astype(F32), *tabs)


def _mm_ln_kernel(a_ref, w_ref, h_ref, g_ref, b_ref, of_ref, ob_ref, *, alpha):
    y = alpha * h_ref[...] + jnp.dot(a_ref[...], w_ref[...], preferred_element_type=F32)
    mu = jnp.mean(y, axis=-1, keepdims=True)
    d = y - mu
    var = jnp.mean(d * d, axis=-1, keepdims=True)
    out = d * lax.rsqrt(var + LN_EPS) * g_ref[...] + b_ref[...]
    of_ref[...] = out
    ob_ref[...] = out.astype(BF16)


def _mm_ln(a, w, h, g, b, *, alpha, layer=None):
    T, K = a.shape
    D = w.shape[-1]
    if layer is None:
        w_spec = pl.BlockSpec((K, D), lambda i: (0, 0), pipeline_mode=pl.Buffered(1))
    else:
        w_spec = pl.BlockSpec((None, K, D), lambda i: (layer, 0, 0), pipeline_mode=pl.Buffered(1))
    row_bytes = 2 * (2 * K) + 2 * (4 * D) + 2 * (4 * D) + 2 * (2 * D) + 4 * D
    budget = VMEM_LIMIT_BYTES - 2 * K * D - VMEM_SLACK_BYTES
    bm = 512
    while bm > 8 and (bm * row_bytes > budget or T % bm):
        bm //= 2
    assert T % bm == 0
    return pl.pallas_call(
        functools.partial(_mm_ln_kernel, alpha=alpha),
        grid=(T // bm,),
        in_specs=[pl.BlockSpec((bm, K), lambda i: (i, 0)),
                  w_spec,
                  pl.BlockSpec((bm, D), lambda i: (i, 0)),
                  pl.BlockSpec((1, D), lambda i: (0, 0)),
                  pl.BlockSpec((1, D), lambda i: (0, 0))],
        out_specs=[pl.BlockSpec((bm, D), lambda i: (i, 0)),
                   pl.BlockSpec((bm, D), lambda i: (i, 0))],
        out_shape=[jax.ShapeDtypeStruct((T, D), F32), jax.ShapeDtypeStruct((T, D), BF16)],
        compiler_params=_cparams(("parallel",)),
        name="mm_ln",
    )(a, w, h, g.reshape(1, D).astype(F32), b.reshape(1, D).astype(F32))


def _ffn_up_kernel(x_ref, wg_ref, wu_ref, o_ref):
    x = x_ref[...]
    g = jnp.dot(x, wg_ref[...], preferred_element_type=F32)
    u = jnp.dot(x, wu_ref[...], preferred_element_type=F32)
    o_ref[...] = ((g / (1.0 + jnp.exp(-g))) * u).astype(o_ref.dtype)


def _ffn_up(x, wg, wu, layer, *, bm=1024, bn=512):
    T, D = x.shape
    Fh = wg.shape[-1]
    assert T % bm == 0 and Fh % bn == 0
    w_spec = pl.BlockSpec((None, D, bn), lambda i, j: (layer, 0, j))
    return pl.pallas_call(
        _ffn_up_kernel,
        grid=(T // bm, Fh // bn),
        in_specs=[pl.BlockSpec((bm, D), lambda i, j: (i, 0)), w_spec, w_spec],
        out_specs=pl.BlockSpec((bm, bn), lambda i, j: (i, j)),
        out_shape=jax.ShapeDtypeStruct((T, Fh), BF16),
        compiler_params=_cparams(("parallel", "parallel")),
        name="ffn_up",
    )(x, wg, wu)


def _chunk_start(c, tk):
    return c * tk if isinstance(c, int) else pl.multiple_of(c * tk, tk)


def _largest_divisor_leq(n, cap):
    return max(d for d in range(1, max(1, min(n, cap)) + 1) if n % d == 0)


def _flash_loop(score_fns, v_ref, s_refs, *, M, tk, body_keys=FLASH_BODY_KEYS):
    S, Dv = v_ref.shape
    n = S // tk
    assert n % 2 == 0
    npairs = n // 2

    def absorb(c, s_ref, mx_cst, carry):
        m, l, acc = carry
        mx, cst = mx_cst
        off = _chunk_start(c, tk)
        m_new = jnp.maximum(m, mx if cst is None else mx + cst)
        a = jnp.exp2(m - m_new)
        m_ref = m_new if cst is None else m_new - cst
        l = a * l
        acc = a * acc
        sub = min(tk, ABSORB_SUB_KEYS)
        for r in range(tk // sub):
            p = jnp.exp2(s_ref[r * sub:(r + 1) * sub, :] - m_ref)
            l = l + jnp.sum(p, axis=0, keepdims=True)
            start = off + r * sub
            if not isinstance(start, int):
                start = pl.multiple_of(start, sub)
            acc = acc + lax.dot_general(v_ref[pl.ds(start, sub), :], p.astype(BF16), (((0,), (0,)), ((), ())),
                                        preferred_element_type=F32)
        return m_new, l, acc

    def pair(scores, issue_next, c0, mx0, state):
        mx1 = scores(c0 + 1, s_refs[1])
        state = absorb(c0, s_refs[0], mx0, state)
        mx0 = issue_next()
        state = absorb(c0 + 1, s_refs[1], mx1, state)
        return mx0, state

    init = (jnp.full((1, M), -jnp.inf, F32), jnp.zeros((1, M), F32), jnp.zeros((Dv, M), F32))
    out = []
    mx0 = score_fns[0](0, s_refs[0])
    for t, scores in enumerate(score_fns):
        state = init
        if npairs > 1:
            def body(jj, carry, scores=scores):
                return pair(scores, lambda: scores(2 * jj + 2, s_refs[0]), 2 * jj, *carry)

            unroll = _largest_divisor_leq(npairs - 1, max(1, body_keys // (2 * tk)))
            mx0, state = lax.fori_loop(0, npairs - 1, body, (mx0, state), unroll=unroll)
        if t + 1 < len(score_fns):
            issue_next = functools.partial(score_fns[t + 1], 0, s_refs[0])
        else:
            issue_next = lambda: None
        mx0, state = pair(scores, issue_next, 2 * (npairs - 1), mx0, state)
        out.append(state[1:])
    return out


def _flash_body(q_tiles, k_ref, v_ref, s_refs, *, tk):
    def make_scores(q):
        qt = q.astype(F32).T.astype(BF16)

        def scores(c, s_ref):
            st = jnp.dot(k_ref[pl.ds(_chunk_start(c, tk), tk), :], qt, preferred_element_type=F32)
            s_ref[...] = st
            return jnp.max(st, axis=0, keepdims=True), None

        return scores

    res = _flash_loop([make_scores(q) for q in q_tiles], v_ref, s_refs, M=q_tiles[0].shape[0], tk=tk)
    return [(acc * (1.0 / l)).T for l, acc in res]


def _gqa_attn_kernel(q_ref, k_ref, v_ref, o_ref, s0_ref, s1_ref, *, G, tq, tk):
    tiles = [slice(r, r + tq) for r in range(0, q_ref.shape[0], tq)]
    qs = [jnp.concatenate([q_ref[rows, g * LANE:(g + 1) * LANE] for g in range(G)], axis=0) for rows in tiles]
    for rows, o in zip(tiles, _flash_body(qs, k_ref, v_ref, (s0_ref, s1_ref), tk=tk)):
        for g in range(G):
            o_ref[rows, g * LANE:(g + 1) * LANE] = o[g * tq:(g + 1) * tq].astype(o_ref.dtype)


def _gqa_attn(q, k, v, *, B, S, tq=128, tk=1024, nt=4):
    H, Hk = q.shape[1] // LANE, k.shape[1] // LANE
    G = H // Hk
    tk = min(tk, S // 2)
    bq = nt * tq
    assert S % bq == 0 and S % tk == 0
    q3, k3, v3 = (t.reshape(B, S, t.shape[1]) for t in (q, k, v))
    kern = functools.partial(_gqa_attn_kernel, G=G, tq=tq, tk=tk)
    out = pl.pallas_call(
        kern,
        grid=(B, Hk, S // bq),
        in_specs=[pl.BlockSpec((None, bq, G * LANE), lambda b, h, i: (b, i, h)),
                  pl.BlockSpec((None, S, LANE), lambda b, h, i: (b, 0, h)),
                  pl.BlockSpec((None, S, LANE), lambda b, h, i: (b, 0, h))],
        out_specs=pl.BlockSpec((None, bq, G * LANE), lambda b, h, i: (b, i, h)),
        out_shape=jax.ShapeDtypeStruct((B, S, H * LANE), BF16),
        scratch_shapes=[pltpu.VMEM((tk, G * tq), F32)] * 2,
        compiler_params=_cparams(("parallel", "parallel", "parallel")),
        name="gqa_attn",
    )(q3, k3, v3)
    return out.reshape(B * S, H * LANE)


def _mla_attn_kernel(q_ref, kn_ref, kr_ref, v_ref, o_ref, kc_ref, s0_ref, s1_ref, *, tq, tk):
    @pl.when(pl.program_id(2) == 0)
    def _():
        kc_ref[:, :LANE] = kn_ref[...]
        kc_ref[:, LANE:] = kr_ref[...]

    tiles = [slice(r, r + tq) for r in range(0, q_ref.shape[0], tq)]
    outs = _flash_body([q_ref[rows, :] for rows in tiles], kc_ref, v_ref, (s0_ref, s1_ref), tk=tk)
    for rows, o in zip(tiles, outs):
        o_ref[rows, :] = o.astype(o_ref.dtype)


def _mla_attn(q, kv, kr, *, B, S, tq=512, tk=1024, nt=4):
    H = q.shape[1] // (2 * LANE)
    tq, tk = min(tq, S // nt), min(tk, S // 2)
    bq = nt * tq
    assert S % bq == 0 and S % tk == 0
    q3, kv3, kr3 = q.reshape(B, S, -1), kv.reshape(B, S, -1), kr.reshape(B, S, LANE)
    kern = functools.partial(_mla_attn_kernel, tq=tq, tk=tk)
    out = pl.pallas_call(
        kern,
        grid=(B, H, S // bq),
        in_specs=[pl.BlockSpec((None, bq, 2 * LANE), lambda b, h, i: (b, i, h)),
                  pl.BlockSpec((None, S, LANE), lambda b, h, i: (b, 0, 2 * h), pipeline_mode=pl.Buffered(1)),
                  pl.BlockSpec((None, S, LANE), lambda b, h, i: (b, 0, 0), pipeline_mode=pl.Buffered(1)),
                  pl.BlockSpec((None, S, LANE), lambda b, h, i: (b, 0, 2 * h + 1))],
        out_specs=pl.BlockSpec((None, bq, LANE), lambda b, h, i: (b, i, h)),
        out_shape=jax.ShapeDtypeStruct((B, S, H * LANE), BF16),
        scratch_shapes=[pltpu.VMEM((S, 2 * LANE), BF16)] + [pltpu.VMEM((tk, tq), F32)] * 2,
        compiler_params=_cparams(("arbitrary", "arbitrary", "arbitrary")),
        name="mla_attn",
    )(q3, kv3, kr3, kv3)
    return out.reshape(B * S, H * LANE)


def _diff_attn_kernel(slopes_ref, q_ref, k_ref, v_ref, lq1_ref, lk1_ref, lq2_ref, lk2_ref, sub_ref, o_ref,
                      s0_ref, s1_ref, t_ref, *, tq, tk, lam_init):
    slope = slopes_ref[pl.program_id(1)]
    arel = slope * (lax.broadcasted_iota(jnp.int32, (tk, tq), 0)
                    - lax.broadcasted_iota(jnp.int32, (tk, tq), 1)).astype(F32)
    t_ref[0] = arel
    t_ref[1] = -arel
    tiles = [slice(r, r + tq) for r in range(0, q_ref.shape[0], tq)]

    def make_scores(t, rows):
        q0 = pl.program_id(2) * q_ref.shape[0] + t * tq
        c_diag = q0 // tk
        t_ref[2 + t] = -jnp.abs(arel + slope * (c_diag * tk - q0).astype(F32))
        qts = [q_ref[rows, c * DIFF_HD:(c + 1) * DIFF_HD].astype(F32).T.astype(BF16) for c in range(2)]

        def scores(c, s_ref):
            off = _chunk_start(c, tk)
            e = slope * (off - q0).astype(F32)
            side = jnp.where(c == c_diag, 2 + t, jnp.where(c > c_diag, 1, 0))
            cst = jnp.where(c == c_diag, 0.0, jnp.where(c > c_diag, -e, e))
            bias = t_ref[side]
            mxs = []
            for comp in range(2):
                ks = k_ref[pl.ds(off, tk), comp * DIFF_HD:(comp + 1) * DIFF_HD]
                st = jnp.dot(ks, qts[comp], preferred_element_type=F32) + bias
                s_ref[:, comp * tq:(comp + 1) * tq] = st
                mxs.append(jnp.max(st, axis=0, keepdims=True))
            return jnp.concatenate(mxs, axis=1), cst

        return scores

    res = _flash_loop([make_scores(t, rows) for t, rows in enumerate(tiles)], v_ref, (s0_ref, s1_ref),
                      M=2 * tq, tk=tk, body_keys=DIFF_BODY_KEYS)
    lam = (jnp.exp(jnp.sum(lq1_ref[...] * lk1_ref[...], axis=-1, keepdims=True))
           - jnp.exp(jnp.sum(lq2_ref[...] * lk2_ref[...], axis=-1, keepdims=True)) + lam_init)
    for rows, (l, acc) in zip(tiles, res):
        on = acc * (1.0 / l)
        o = (on[:, :tq] - lam * on[:, tq:]).T
        ms = jnp.mean(o * o, axis=-1, keepdims=True)
        o = o * lax.rsqrt(ms + RMS_EPS) * sub_ref[...] * (1.0 - lam_init)
        o_ref[rows, :] = o.astype(o_ref.dtype)


def _diff_attn(q, kv, lam_vecs, subln, *, B, S, lam_init, tq=256, tk=512, nt=4):
    W = 2 * DIFF_HD
    H = q.shape[1] // W
    tq, tk = min(tq, S // nt), min(tk, S // 2)
    bq = nt * tq
    assert S % bq == 0 and S % tk == 0 and tk % tq == 0
    slopes = LOG2E * jnp.exp2(-8.0 * jnp.arange(1, H + 1, dtype=F32) / H)
    q3, kv3 = q.reshape(B, S, -1), kv.reshape(B, S, -1)
    vec = pl.BlockSpec((1, DIFF_HD), lambda b, h, i: (0, 0))
    kern = functools.partial(_diff_attn_kernel, tq=tq, tk=tk, lam_init=lam_init)
    out = pl.pallas_call(
        kern,
        grid=(B, H, S // bq),
        in_specs=[pl.BlockSpec(memory_space=pltpu.SMEM),
                  pl.BlockSpec((None, bq, W), lambda b, h, i: (b, i, h)),
                  pl.BlockSpec((None, S, W), lambda b, h, i: (b, 0, h)),
                  pl.BlockSpec((None, S, W), lambda b, h, i: (b, 0, H + h)),
                  vec, vec, vec, vec,
                  pl.BlockSpec((1, W), lambda b, h, i: (0, 0))],
        out_specs=pl.BlockSpec((None, bq, W), lambda b, h, i: (b, i, h)),
        out_shape=jax.ShapeDtypeStruct((B, S, H * W), BF16),
        scratch_shapes=[pltpu.VMEM((tk, 2 * tq), F32)] * 2
                       + [pltpu.VMEM((2 + nt, tk, tq), F32)],
        compiler_params=_cparams(("parallel", "parallel", "parallel")),
        name="diff_attn",
    )(slopes, q3, kv3, kv3, *[v.reshape(1, DIFF_HD).astype(F32) for v in lam_vecs],
      subln.reshape(1, W).astype(F32))
    return out.reshape(B * S, H * W)


def _na_rep_block(p, nblk):
    return jnp.where(p < 3, p, nblk - NA_WIN_BLOCKS + p)


def _na_bias_kernel(rpbT_ref, o_ref, *, nblk, R):
    nk = NA_WIN_BLOCKS * Q_BLOCK
    rows_per_blk = Q_BLOCK // GRID_W
    kh = min(NA_KH_MAX, R)
    j = _na_rep_block(pl.program_id(0), nblk)
    ws = jnp.clip(j - 2, 0, nblk - NA_WIN_BLOCKS)
    tab = rpbT_ref[...]
    n_dr = 2 * NA_KH_MAX - 1
    n_dc = 2 * NA_KW - 1

    w_shift = GRID_W.bit_length() - 1
    n1 = lax.broadcasted_iota(jnp.int32, (NA_DC_PAD, nk), 1)
    krow1 = rows_per_blk * ws + (n1 >> w_shift)
    halves = []
    for a in range(rows_per_blk):
        qrow = rows_per_blk * j + a
        drow1 = jnp.clip(krow1 - qrow + NA_KH_MAX - 1, 0, n_dr - 1)
        sel = jnp.zeros((NA_DC_PAD, nk), F32)
        for dr in range(n_dr):
            sel = jnp.where(drow1 == dr, tab[:, dr:dr + 1], sel)
        qc = lax.broadcasted_iota(jnp.int32, (GRID_W, nk), 0)
        n = lax.broadcasted_iota(jnp.int32, (GRID_W, nk), 1)
        kc = n & (GRID_W - 1)
        krow = rows_per_blk * ws + (n >> w_shift)
        rs = jnp.clip(qrow - kh // 2, 0, R - kh)
        cs = jnp.clip(qc - NA_KW // 2, 0, GRID_W - NA_KW)
        ok = (kc >= cs) & (kc < cs + NA_KW) & (krow >= rs) & (krow < rs + kh)
        dcol = jnp.clip(kc - qc + NA_KW - 1, 0, n_dc - 1)
        bias = jnp.zeros((GRID_W, nk), F32)
        for dc in range(n_dc):
            bias = jnp.where(dcol == dc, sel[dc:dc + 1, :], bias)
        halves.append(jnp.where(ok, bias * LOG2E, NEG_INF))
    o_ref[...] = jnp.concatenate(halves, axis=0).T


def _na_bias(rpb, *, S):
    H = rpb.shape[0]
    nblk, R = S // Q_BLOCK, S // GRID_W
    nk = NA_WIN_BLOCKS * Q_BLOCK
    rpbT = jnp.transpose(rpb.astype(F32), (0, 2, 1))
    rpbT = jnp.pad(rpbT, ((0, 0), (0, NA_DC_PAD - rpbT.shape[1]), (0, LANE - rpbT.shape[2])))
    kern = functools.partial(_na_bias_kernel, nblk=nblk, R=R)
    return pl.pallas_call(
        kern,
        grid=(NA_WIN_BLOCKS, H),
        in_specs=[pl.BlockSpec((None, NA_DC_PAD, LANE), lambda p, h: (h, 0, 0))],
        out_specs=pl.BlockSpec((None, None, nk, Q_BLOCK), lambda p, h: (p, h, 0, 0)),
        out_shape=jax.ShapeDtypeStruct((NA_WIN_BLOCKS, H, nk, Q_BLOCK), F32),
        compiler_params=_cparams(("parallel", "parallel")),
        name="na_bias",
    )(rpbT)


def _na_attn_kernel(q_ref, *refs, H):
    k_refs = refs[:NA_WIN_BLOCKS]
    v_refs = refs[NA_WIN_BLOCKS:2 * NA_WIN_BLOCKS]
    bias_ref, o_ref = refs[2 * NA_WIN_BLOCKS], refs[2 * NA_WIN_BLOCKS + 1]
    for h in range(H):
        hs = slice(h * LANE, (h + 1) * LANE)
        k = jnp.concatenate([r[:, hs] for r in k_refs], axis=0)
        v = jnp.concatenate([r[:, hs] for r in v_refs], axis=0)
        st = lax.dot_general(k, q_ref[:, hs], (((1,), (1,)), ((), ())), preferred_element_type=F32)
        st = st + bias_ref[h]
        m = jnp.max(st, axis=0, keepdims=True)
        p = jnp.exp2(st - m)
        l = jnp.sum(p, axis=0, keepdims=True)
        ot = lax.dot_general(v, p.astype(BF16), (((0,), (0,)), ((), ())), preferred_element_type=F32)
        o_ref[:, hs] = (ot * (1.0 / l)).T.astype(o_ref.dtype)


def _na_attn(q, kv, bias, *, B, S):
    D = q.shape[1]
    H = D // LANE
    nblk = S // Q_BLOCK
    assert S % Q_BLOCK == 0 and nblk >= NA_WIN_BLOCKS
    q3, kv3 = q.reshape(B, S, D), kv.reshape(B, S, 2 * D)

    def win(j):
        return jnp.clip(j - 2, 0, nblk - NA_WIN_BLOCKS)

    def pattern(j):
        return jnp.where(j < 2, j, jnp.where(j <= nblk - 3, 2, j - (nblk - NA_WIN_BLOCKS)))

    k_specs = [pl.BlockSpec((None, Q_BLOCK, D), lambda b, j, t=t: (b, win(j) + t, 0))
               for t in range(NA_WIN_BLOCKS)]
    v_specs = [pl.BlockSpec((None, Q_BLOCK, D), lambda b, j, t=t: (b, win(j) + t, 1))
               for t in range(NA_WIN_BLOCKS)]
    nk = NA_WIN_BLOCKS * Q_BLOCK
    out = pl.pallas_call(
        functools.partial(_na_attn_kernel, H=H),
        grid=(B, nblk),
        in_specs=[pl.BlockSpec((None, Q_BLOCK, D), lambda b, j: (b, j, 0))] + k_specs + v_specs
                 + [pl.BlockSpec((None, H, nk, Q_BLOCK), lambda b, j: (pattern(j), 0, 0, 0))],
        out_specs=pl.BlockSpec((None, Q_BLOCK, D), lambda b, j: (b, j, 0)),
        out_shape=jax.ShapeDtypeStruct((B, S, D), BF16),
        compiler_params=_cparams(("parallel", "parallel")),
        name="na_attn",
    )(q3, *([kv3] * (2 * NA_WIN_BLOCKS)), bias)
    return out.reshape(B * S, D)


def _rope_cos_sin(pos, dim):
    inv = 1.0 / (ROPE_THETA ** (jnp.arange(0, dim, 2, dtype=F32) / dim))
    ang = pos.astype(F32)[:, None] * inv[None, :]
    return jnp.cos(ang), jnp.sin(ang)


def _rope_tables(pairs, S):
    z = jnp.zeros((S, ROT_HALF), F32)
    c_parts, s1_parts, s2_parts = [], [], []
    for cos, sin in pairs:
        c_parts += [cos, cos]
        s1_parts += [-sin, z]
        s2_parts += [z, sin]
    pad = LANE - 2 * ROT_HALF * len(pairs)
    if pad:
        zp = jnp.zeros((S, pad), F32)
        c_parts.append(zp), s1_parts.append(zp), s2_parts.append(zp)
    return tuple(jnp.concatenate(p, axis=1) for p in (c_parts, s1_parts, s2_parts))


def _mla_mixer(h, hb, B, S, w_in, q_norm, w_q_up, kv_norm, w_kv_up):
    QR, KVR = q_norm.shape[0], kv_norm.shape[0]
    H = w_q_up.shape[1] // (MLA_NOPE + MLA_ROPE)
    tabs = _rope_tables([_rope_cos_sin(jnp.arange(S), MLA_ROPE)], S)
    w_in = jnp.pad(w_in.astype(BF16), ((0, 0), (0, LANE - MLA_ROPE)))
    cq, ckv, kr = _mla_down(hb, w_in, q_norm, kv_norm, tabs, seq=S)
    wq = w_q_up.astype(BF16).reshape(QR, H, MLA_NOPE + MLA_ROPE)
    wq = jnp.pad(wq, ((0, 0), (0, 0), (0, 2 * LANE - MLA_NOPE - MLA_ROPE))).reshape(QR, H * 2 * LANE)
    q = _proj(cq, wq, seq=S, rope="odd", tabs=tabs, scale=LOG2E * (MLA_NOPE + MLA_ROPE) ** -0.5)
    kv = _proj(ckv, w_kv_up.astype(BF16), seq=S)
    return _mla_attn(q, kv, kr, B=B, S=S)


def _gqa_mixer(h, hb, B, S, w_qkv, q_norm, k_norm):
    HD = q_norm.shape[0]
    Hk = GQA_KV_HEADS
    nq = w_qkv.shape[1] - 2 * Hk * HD
    half = HD // 2
    pos = jnp.arange(S)
    tabs = _rope_tables([_rope_cos_sin(pos // GRID_W, half), _rope_cos_sin(pos % GRID_W, half)], S)
    w = w_qkv.astype(BF16)
    q = _proj(hb, w[:, :nq], seq=S, norm="group", gain=q_norm, rope="all", tabs=tabs, scale=LOG2E * HD ** -0.5)
    k = _proj(hb, w[:, nq:nq + Hk * HD], seq=S, norm="group", gain=k_norm, rope="all", tabs=tabs)
    v = _proj(hb, w[:, nq + Hk * HD:], seq=S)
    return _gqa_attn(q, k, v, B=B, S=S)


def _na_mixer(h, hb, B, S, w_qkv, rpb):
    D = w_qkv.shape[1] // 3
    w = w_qkv.astype(BF16)
    q = _proj(hb, w[:, :D], seq=S, scale=LOG2E * LANE ** -0.5)
    kv = _proj(hb, w[:, D:], seq=S)
    return _na_attn(q, kv, _na_bias(rpb, S=S), B=B, S=S)


def _diff_mixer(h, hb, B, S, w_qkv, lam_vecs, subln, lam_init):
    D = w_qkv.shape[1] // 3
    w = w_qkv.astype(BF16)
    q = _proj(hb, w[:, :D], seq=S, scale=LOG2E * DIFF_HD ** -0.5)
    kv = _proj(hb, w[:, D:], seq=S)
    return _diff_attn(q, kv, lam_vecs, subln, B=B, S=S, lam_init=lam_init)


def kernel(x, mla_w_in, mla_q_norm, mla_w_q_up, mla_kv_norm, mla_w_kv_up, mla_w_o, gqa_w_qkv, gqa_q_norm, gqa_k_norm, gqa_w_o, na_w_qkv, na_rpb, na_w_o, diff_w_qkv, diff_lambda_q1, diff_lambda_k1, diff_lambda_q2, diff_lambda_k2, diff_subln, diff_w_o, ffn_w_gate, ffn_w_up, ffn_w_down, ln_mix_g, ln_mix_b, ln_ffn_g, ln_ffn_b):
    B, S, D = x.shape
    depth = ffn_w_gate.shape[0]
    alpha = (2.0 * depth) ** 0.25
    h = x.reshape(B * S, D)
    hb = h
    wg_all, wu_all, wd_all = (w.astype(BF16) for w in (ffn_w_gate, ffn_w_up, ffn_w_down))
    for i in range(depth):
        m = i % 4
        if m == 0:
            o = _mla_mixer(h, hb, B, S, mla_w_in, mla_q_norm, mla_w_q_up, mla_kv_norm, mla_w_kv_up)
            w_o = mla_w_o
        elif m == 1:
            o = _gqa_mixer(h, hb, B, S, gqa_w_qkv, gqa_q_norm, gqa_k_norm)
            w_o = gqa_w_o
        elif m == 2:
            o = _na_mixer(h, hb, B, S, na_w_qkv, na_rpb)
            w_o = na_w_o
        else:
            lam_init = 0.8 - 0.6 * math.exp(-0.3 * i)
            o = _diff_mixer(h, hb, B, S, diff_w_qkv,
                            (diff_lambda_q1, diff_lambda_k1, diff_lambda_q2, diff_lambda_k2),
                            diff_subln, lam_init)
            w_o = diff_w_o
        h, hb = _mm_ln(o, w_o.astype(BF16), h, ln_mix_g[i], ln_mix_b[i], alpha=alpha)
        hid = _ffn_up(hb, wg_all, wu_all, i)
        h, hb = _mm_ln(hid, wd_all, h, ln_ffn_g[i], ln_ffn_b[i], alpha=alpha, layer=i)
    return h.reshape(B, S, D)
```
